```python
import jax, jax.numpy as jnp
from jax import lax
import numpy as np

D_MODEL = 1024
BATCH = 2
SEQ = 8192
DEPTH = 1

EPS = 1e-6
CONV_WIDTH = D_MODEL
CONV_K = 3
GDN_HEADS = 8
GDN_DK = 128
GDN_DV = 128
GDN_CONV_K = 4
GDN_CHUNK = 64
GDN_QK = GDN_HEADS * GDN_DK
GDN_VW = GDN_HEADS * GDN_DV
D_FF = 256 * ((8 * D_MODEL // 3 + 255) // 256)
FFN_CONV_K = 3
IN_SPLITS = (CONV_WIDTH, CONV_WIDTH, CONV_WIDTH, GDN_QK, GDN_QK, GDN_VW, GDN_VW, GDN_HEADS, GDN_HEADS, D_MODEL, D_MODEL)
IN_WIDTH = sum(IN_SPLITS)

kernel_name = "hybrid_shortconv_gdn_gated_merge_convffn"


def _split_points():
    pts, acc = [], 0
    for w in IN_SPLITS[:-1]:
        acc += w
        pts.append(acc)
    return pts


def rmsnorm(x, g):
    xf = x.astype(jnp.float32)
    y = xf * lax.rsqrt(jnp.mean(xf * xf, axis=-1, keepdims=True) + EPS)
    return (y * g.astype(jnp.float32)).astype(x.dtype)


def l2norm(x):
    return x * lax.rsqrt(jnp.sum(x * x, axis=-1, keepdims=True) + EPS)


def causal_dwconv(x, w):
    K = w.shape[0]
    S = x.shape[1]
    xp = jnp.pad(x, ((0, 0), (K - 1, 0), (0, 0)))
    return sum(xp[:, j:j + S] * w[j] for j in range(K))


def short_conv_branch(bg, cg, xv, conv_w):
    return bg * causal_dwconv(cg * xv, conv_w)


def gated_delta_rule_chunked(q, k, v, g, beta):
    Bsz, S, H, dk = q.shape
    dv = v.shape[-1]
    C = GDN_CHUNK
    N = S // C

    def chunks(t):
        return t.reshape(Bsz, N, C, H, *t.shape[3:]).swapaxes(2, 3)

    q, k, v, g, beta = (chunks(t) for t in (q, k, v, g, beta))
    G = jnp.cumsum(g, axis=-1)
    tril = jnp.tril(jnp.ones((C, C), dtype=bool))
    strict = jnp.tril(jnp.ones((C, C), dtype=bool), -1)
    decay = jnp.exp(jnp.where(tril, G[..., :, None] - G[..., None, :], -jnp.inf))
    k_beta = k * beta[..., None]
    v_beta = v * beta[..., None]
    L = jnp.where(strict, jnp.einsum('bnhid,bnhjd->bnhij', k_beta, k) * decay, 0.0)
    eye = jnp.eye(C, dtype=q.dtype)
    rhs = jnp.concatenate([v_beta, k_beta * jnp.exp(G)[..., None]], axis=-1)
    sol = lax.linalg.triangular_solve(eye + L, rhs, left_side=True, lower=True)
    u, w = sol[..., :dv], sol[..., dv:]
    A_qk = jnp.einsum('bnhid,bnhjd->bnhij', q, k) * decay
    q_dec = q * jnp.exp(G)[..., None]
    G_last = G[..., -1:]
    k_dec = k * jnp.exp(G_last - G)[..., None]

    def step(state, inp):
        q_c, k_c, u_c, w_c, A_c, gl = inp
        v_new = u_c - jnp.einsum('bhcd,bhde->bhce', w_c, state)
        o = jnp.einsum('bhcd,bhde->bhce', q_c, state) + jnp.einsum('bhij,bhje->bhie', A_c, v_new)
        state = state * jnp.exp(gl)[..., None] + jnp.einsum('bhcd,bhce->bhde', k_c, v_new)
        return state, o

    xs = tuple(jnp.moveaxis(t, 1, 0) for t in (q_dec, k_dec, u, w, A_qk, G_last))
    state0 = jnp.zeros((Bsz, H, dk, dv), dtype=q.dtype)
    _, o = lax.scan(step, state0, xs)
    return o.transpose(1, 0, 3, 2, 4).reshape(Bsz, S, H, dv)


def gdn_branch(q, k, v, z, a, b, conv_w, A_log, dt_bias, norm_g):
    Bsz, S, _ = q.shape
    dtype = q.dtype
    f32 = jnp.float32
    qkv = jax.nn.silu(causal_dwconv(jnp.concatenate([q, k, v], axis=-1), conv_w))
    q, k, v = jnp.split(qkv, [GDN_QK, 2 * GDN_QK], axis=-1)
    q = l2norm(q.reshape(Bsz, S, GDN_HEADS, GDN_DK).astype(f32)) * (GDN_DK ** -0.5)
    k = l2norm(k.reshape(Bsz, S, GDN_HEADS, GDN_DK).astype(f32))
    v = v.reshape(Bsz, S, GDN_HEADS, GDN_DV).astype(f32)
    beta = jax.nn.sigmoid(b.astype(f32))
    g = -jnp.exp(A_log.astype(f32)) * jax.nn.softplus(a.astype(f32) + dt_bias.astype(f32))
    o = gated_delta_rule_chunked(q, k, v, g, beta)
    o = o * lax.rsqrt(jnp.mean(o * o, axis=-1, keepdims=True) + EPS) * norm_g.astype(f32)
    o = o * jax.nn.silu(z.reshape(Bsz, S, GDN_HEADS, GDN_DV).astype(f32))
    return o.reshape(Bsz, S, GDN_VW).astype(dtype)


def conv_ffn(h, w_up, conv_w, w_down):
    up = causal_dwconv(h @ w_up, conv_w)
    gate, val = jnp.split(up, [D_FF], axis=-1)
    return (jax.nn.silu(gate) * val) @ w_down


def setup_inputs(seed: int = 0) -> dict:
    key = jax.random.key(seed)
    ks = jax.random.split(key, 16)
    f32 = jnp.float32
    nrm = lambda k, shape, scale: jax.random.normal(k, shape, f32) * scale
    x = jax.random.normal(ks[0], (BATCH, SEQ, D_MODEL), f32)
    norm_mix_g = 1.0 + nrm(ks[1], (DEPTH, D_MODEL), 0.02)
    w_in = nrm(ks[2], (DEPTH, D_MODEL, IN_WIDTH), D_MODEL ** -0.5)
    conv_a_w = nrm(ks[3], (DEPTH, CONV_K, CONV_WIDTH), CONV_K ** -0.5)
    gdn_conv_w = nrm(ks[4], (DEPTH, GDN_CONV_K, 2 * GDN_QK + GDN_VW), GDN_CONV_K ** -0.5)
    gdn_A_log = jnp.log(jax.random.uniform(ks[5], (DEPTH, GDN_HEADS), f32, 1.0, 16.0))
    dt = jnp.exp(jax.random.uniform(ks[6], (DEPTH, GDN_HEADS), f32, np.log(1e-3), np.log(1e-1)))
    gdn_dt_bias = dt + jnp.log(-jnp.expm1(-dt))
    gdn_norm_g = 1.0 + nrm(ks[7], (DEPTH, GDN_DV), 0.02)
    w_a_out = nrm(ks[8], (DEPTH, CONV_WIDTH, D_MODEL), CONV_WIDTH ** -0.5)
    w_b_out = nrm(ks[9], (DEPTH, GDN_VW, D_MODEL), GDN_VW ** -0.5)
    w_o = nrm(ks[10], (DEPTH, D_MODEL, D_MODEL), D_MODEL ** -0.5)
    norm_ffn_g = 1.0 + nrm(ks[11], (DEPTH, D_MODEL), 0.02)
    w_up = nrm(ks[12], (DEPTH, D_MODEL, 2 * D_FF), D_MODEL ** -0.5)
    ffn_conv_w = nrm(ks[13], (DEPTH, FFN_CONV_K, 2 * D_FF), FFN_CONV_K ** -0.5)
    w_down = nrm(ks[14], (DEPTH, D_FF, D_MODEL), D_FF ** -0.5)
    norm_final_g = 1.0 + nrm(ks[15], (D_MODEL,), 0.02)
    return {"x": x, "norm_mix_g": norm_mix_g, "w_in": w_in, "conv_a_w": conv_a_w,
            "gdn_conv_w": gdn_conv_w, "gdn_A_log": gdn_A_log, "gdn_dt_bias": gdn_dt_bias,
            "gdn_norm_g": gdn_norm_g, "w_a_out": w_a_out, "w_b_out": w_b_out, "w_o": w_o,
            "norm_ffn_g": norm_ffn_g, "w_up": w_up, "ffn_conv_w": ffn_conv_w, "w_down": w_down,
            "norm_final_g": norm_final_g}


def reference(x, norm_mix_g, w_in, conv_a_w, gdn_conv_w, gdn_A_log, gdn_dt_bias, gdn_norm_g,
              w_a_out, w_b_out, w_o, norm_ffn_g, w_up, ffn_conv_w, w_down, norm_final_g):
    pts = _split_points()
    for l in range(DEPTH):
        h = rmsnorm(x, norm_mix_g[l])
        proj = h @ w_in[l]
        bg, cg, xv, q, k, v, z, a, b, ga, gb = jnp.split(proj, pts, axis=-1)
        y_a = short_conv_branch(bg, cg, xv, conv_a_w[l]) @ w_a_out[l]
        y_b = gdn_branch(q, k, v, z, a, b, gdn_conv_w[l], gdn_A_log[l], gdn_dt_bias[l],
                         gdn_norm_g[l]) @ w_b_out[l]
        mix = jax.nn.sigmoid(ga) * y_a + jax.nn.sigmoid(gb) * y_b
        x = x + mix @ w_o[l]
        h = rmsnorm(x, norm_ffn_g[l])
        x = x + conv_ffn(h, w_up[l], ffn_conv_w[l], w_down[l])
    return rmsnorm(x, norm_final_g)
```

```python
import functools

import jax
import jax.numpy as jnp
from jax import lax
from jax.experimental import pallas as pl
from jax.experimental.pallas import tpu as pltpu

D_MODEL = 1024
HEADS = 8
HEAD_DIM = 128
GDN_CONV_K = 4
CONV_K = 3
EPS = 1e-6
LANES = 128
CARRY_ROWS = 8
SEQ_TILE = 256
CHUNK = 64
VMEM_LIMIT = 52 * 1024 * 1024

F32 = jnp.float32
BF16 = jnp.bfloat16


def _dot(a, b):
    return jnp.dot(a, b, preferred_element_type=F32)


def _dot_nt(a, b):
    return lax.dot_general(a, b, (((1,), (1,)), ((), ())), preferred_element_type=F32)


def _dot_tn(a, b):
    return lax.dot_general(a, b, (((0,), (0,)), ((), ())), preferred_element_type=F32)


def _sigmoid(x):
    return 1.0 / (1.0 + jnp.exp(-x))


def _rmsnorm(x, g):
    return x * lax.rsqrt(jnp.mean(x * x, axis=-1, keepdims=True) + EPS) * g


def _causal_conv(buf, w, taps, rows):
    acc = None
    for j in range(taps):
        start = CARRY_ROWS - taps + 1 + j
        term = w[j:j + 1, :] * buf[start:start + rows, :]
        acc = term if acc is None else acc + term
    return acc


def _split3(x):
    hi = x.astype(BF16)
    r = x - hi.astype(F32)
    mid = r.astype(BF16)
    lo = (r - mid.astype(F32)).astype(BF16)
    return hi, mid, lo


def _in_proj_kernel(x_ref, ng_ref, wconv_ref, wqkv_ref, wz_ref, wab_ref, wgate_ref, waout_ref,
                    cwa_ref, cwg_ref, alog_ref, dtb_ref,
                    mixa_ref, sgb_ref, zs_ref, q_ref, k_ref, v_ref, gb_ref,
                    cbuf, qbuf):
    ts = x_ref.shape[1]

    @pl.when(pl.program_id(1) == 0)
    def _():
        cbuf[0:CARRY_ROWS, :] = jnp.zeros((CARRY_ROWS, cbuf.shape[1]), F32)
        qbuf[0:CARRY_ROWS, :] = jnp.zeros((CARRY_ROWS, qbuf.shape[1]), F32)

    h = _rmsnorm(x_ref[0], ng_ref[...]).astype(BF16)

    p = _dot(h, wconv_ref[...])
    cbuf[CARRY_ROWS:CARRY_ROWS + ts, :] = p[:, D_MODEL:2 * D_MODEL] * p[:, 2 * D_MODEL:]
    conv = _causal_conv(cbuf, cwa_ref[...], CONV_K, ts)
    cbuf[0:CARRY_ROWS, :] = cbuf[ts:ts + CARRY_ROWS, :]
    y_a = _dot((p[:, :D_MODEL] * conv).astype(BF16), waout_ref[...])
    pg = _dot(h, wgate_ref[...])
    mixa_ref[0] = _sigmoid(pg[:, :D_MODEL]) * y_a
    sgb_ref[0] = _sigmoid(pg[:, D_MODEL:])

    pz = _dot(h, wz_ref[...])
    zs_ref[0] = pz * _sigmoid(pz)

    pab = _dot(h, wab_ref[...])
    sp = pab + dtb_ref[...]
    softplus = jnp.maximum(sp, 0.0) + jnp.log1p(jnp.exp(-jnp.abs(sp)))
    lane = lax.broadcasted_iota(jnp.int32, pab.shape, 1)
    gb_ref[0] = jnp.where(lane < HEADS, -jnp.exp(alog_ref[...]) * softplus, _sigmoid(pab))

    qbuf[CARRY_ROWS:CARRY_ROWS + ts, :] = _dot(h, wqkv_ref[...])
    c = _causal_conv(qbuf, cwg_ref[...], GDN_CONV_K, ts)
    qbuf[0:CARRY_ROWS, :] = qbuf[ts:ts + CARRY_ROWS, :]
    c = c * _sigmoid(c)
    for hd in range(HEADS):
        lo = hd * HEAD_DIM
        qh = c[:, lo:lo + HEAD_DIM]
        kh = c[:, D_MODEL + lo:D_MODEL + lo + HEAD_DIM]
        q_ref[0, :, lo:lo + HEAD_DIM] = qh * (
            lax.rsqrt(jnp.sum(qh * qh, axis=-1, keepdims=True) + EPS) * (HEAD_DIM ** -0.5))
        k_ref[0, :, lo:lo + HEAD_DIM] = kh * lax.rsqrt(jnp.sum(kh * kh, axis=-1, keepdims=True) + EPS)
    v_ref[0] = c[:, 2 * D_MODEL:]


def _gdn_kernel(x_ref, mixa_ref, sgb_ref, zs_ref, q_ref, k_ref, v_ref, gb_ref, gng_ref, wb_ref, wo_ref,
                x1_ref, state, ogbuf):
    ts = x_ref.shape[1]

    @pl.when(pl.program_id(1) == 0)
    def _():
        state[...] = jnp.zeros(state.shape, F32)

    row = lax.broadcasted_iota(jnp.int32, (ts, ts), 0)
    col = lax.broadcasted_iota(jnp.int32, (ts, ts), 1)
    same = (row // CHUNK) == (col // CHUNK)
    tril = same & (col <= row)
    strict = same & (col < row)
    eye = (row == col).astype(F32)

    gbv = gb_ref[0]
    tril_b = tril.astype(BF16)
    same_b = same.astype(BF16)
    G = None
    GL = None
    for part in _split3(gbv):
        G = _dot(tril_b, part) if G is None else G + _dot(tril_b, part)
        GL = _dot(same_b, part) if GL is None else GL + _dot(same_b, part)
    GT = G.T
    eG = jnp.exp(G)
    eGLG = jnp.exp(GL - G)
    eGL = jnp.exp(GL)
    gng = gng_ref[...]

    for hd in range(HEADS):
        lo = hd * HEAD_DIM
        q = q_ref[0, :, lo:lo + HEAD_DIM]
        k = k_ref[0, :, lo:lo + HEAD_DIM]
        v = v_ref[0, :, lo:lo + HEAD_DIM]
        beta = gbv[:, HEADS + hd:HEADS + hd + 1]
        decay = jnp.where(tril, jnp.exp(G[:, hd:hd + 1] - GT[hd:hd + 1, :]), 0.0)
        eg = eG[:, hd:hd + 1]
        kb = k * beta
        qd = (q * eg).astype(BF16)
        kd = (k * eGLG[:, hd:hd + 1]).astype(BF16)
        k_b = k.astype(BF16)
        L = jnp.where(strict, _dot_nt(kb.astype(BF16), k_b) * decay, 0.0)
        A = (_dot_nt(q.astype(BF16), k_b) * decay).astype(BF16)

        T = eye - L
        P = L.astype(BF16)
        n = 2
        while n < CHUNK:
            P = _dot(P, P).astype(BF16)
            T = T + _dot(T.astype(BF16), P)
            n *= 2
        uw = _dot(T.astype(BF16), jnp.concatenate([v * beta, kb * eg], axis=1).astype(BF16))
        u = uw[:, :HEAD_DIM]
        w = uw[:, HEAD_DIM:].astype(BF16)

        S = state[hd]
        for c in range(ts // CHUNK):
            r0 = c * CHUNK
            ws = _dot(jnp.concatenate([w[r0:r0 + CHUNK], qd[r0:r0 + CHUNK]], axis=0), S.astype(BF16))
            vn = (u[r0:r0 + CHUNK] - ws[:CHUNK]).astype(BF16)
            o = ws[CHUNK:] + _dot(A[r0:r0 + CHUNK, r0:r0 + CHUNK], vn)
            S = S * eGL[r0:r0 + 1, hd:hd + 1] + _dot_tn(kd[r0:r0 + CHUNK], vn)
            o = o * lax.rsqrt(jnp.mean(o * o, axis=-1, keepdims=True) + EPS) * gng
            ogbuf[r0:r0 + CHUNK, lo:lo + HEAD_DIM] = (
                o * zs_ref[0, r0:r0 + CHUNK, lo:lo + HEAD_DIM]).astype(BF16)
        state[hd] = S

    y_b = _dot(ogbuf[...], wb_ref[...])
    mix = mixa_ref[0] + sgb_ref[0] * y_b
    x1_ref[0] = x_ref[0] + _dot(mix.astype(BF16), wo_ref[...])


def _ffn_kernel(x1_ref, ng_ref, wup_ref, cw_ref, wdown_ref, nf_ref, out_ref, ubuf, *, final_norm):
    ts = x1_ref.shape[1]
    d_ff = wdown_ref.shape[0]

    @pl.when(pl.program_id(1) == 0)
    def _():
        ubuf[0:CARRY_ROWS, :] = jnp.zeros((CARRY_ROWS, ubuf.shape[1]), F32)

    x1 = x1_ref[0]
    h = _rmsnorm(x1, ng_ref[...]).astype(BF16)
    ubuf[CARRY_ROWS:CARRY_ROWS + ts, :] = _dot(h, wup_ref[...])
    up = _causal_conv(ubuf, cw_ref[...], CONV_K, ts)
    ubuf[0:CARRY_ROWS, :] = ubuf[ts:ts + CARRY_ROWS, :]
    gate = up[:, :d_ff]
    act = (gate * _sigmoid(gate) * up[:, d_ff:]).astype(BF16)
    x2 = x1 + _dot(act, wdown_ref[...])
    out_ref[0] = _rmsnorm(x2, nf_ref[...]) if final_norm else x2


def _const_spec(shape):
    return pl.BlockSpec(shape, lambda b, s: (0,) * len(shape), pipeline_mode=pl.Buffered(1))


def _tile_spec(width, ts):
    return pl.BlockSpec((1, ts, width), lambda b, s: (b, s, 0))


def _params():
    return pltpu.CompilerParams(dimension_semantics=("arbitrary", "arbitrary"),
                                vmem_limit_bytes=VMEM_LIMIT)


def kernel(x, norm_mix_g, w_in, conv_a_w, gdn_conv_w, gdn_A_log, gdn_dt_bias, gdn_norm_g,
           w_a_out, w_b_out, w_o, norm_ffn_g, w_up, ffn_conv_w, w_down, norm_final_g):
    B, S, D = x.shape
    depth = w_in.shape[0]
    ts = SEQ_TILE
    grid = (B, S // ts)
    d_ff = w_down.shape[1]
    tok = jax.ShapeDtypeStruct((B, S, D), F32)

    for l in range(depth):
        wi = w_in[l].astype(BF16)
        w_conv = wi[:, :3 * D]
        w_qkv = wi[:, 3 * D:6 * D]
        w_z = wi[:, 6 * D:7 * D]
        w_ab = jnp.pad(wi[:, 7 * D:7 * D + 2 * HEADS], ((0, 0), (0, LANES - 2 * HEADS)))
        w_gate = wi[:, 7 * D + 2 * HEADS:]
        alog = jnp.pad(gdn_A_log[l], (0, LANES - HEADS)).reshape(1, LANES)
        dtb = jnp.pad(gdn_dt_bias[l], (0, LANES - HEADS)).reshape(1, LANES)

        mixa, sgb, zs, q, k, v, gb = pl.pallas_call(
            _in_proj_kernel,
            grid=grid,
            in_specs=[_tile_spec(D, ts), _const_spec((1, D)),
                      _const_spec((D, 3 * D)), _const_spec((D, 3 * D)), _const_spec((D, D)),
                      _const_spec((D, LANES)), _const_spec((D, 2 * D)), _const_spec((D, D)),
                      _const_spec((CONV_K, D)), _const_spec((GDN_CONV_K, 3 * D)),
                      _const_spec((1, LANES)), _const_spec((1, LANES))],
            out_specs=[_tile_spec(D, ts)] * 6 + [_tile_spec(LANES, ts)],
            out_shape=[tok] * 6 + [jax.ShapeDtypeStruct((B, S, LANES), F32)],
            scratch_shapes=[pltpu.VMEM((ts + CARRY_ROWS, D), F32),
                            pltpu.VMEM((ts + CARRY_ROWS, 3 * D), F32)],
            compiler_params=_params(),
            name="in_proj",
        )(x, norm_mix_g[l].reshape(1, D), w_conv, w_qkv, w_z, w_ab, w_gate, w_a_out[l].astype(BF16),
          conv_a_w[l], gdn_conv_w[l], alog, dtb)

        x = pl.pallas_call(
            _gdn_kernel,
            grid=grid,
            in_specs=[_tile_spec(D, ts)] * 7 + [_tile_spec(LANES, ts), _const_spec((1, HEAD_DIM)),
                                                 _const_spec((D, D)), _const_spec((D, D))],
            out_specs=_tile_spec(D, ts),
            out_shape=tok,
            scratch_shapes=[pltpu.VMEM((HEADS, HEAD_DIM, HEAD_DIM), F32),
                            pltpu.VMEM((ts, D), BF16)],
            compiler_params=_params(),
            name="gdn_mix",
        )(x, mixa, sgb, zs, q, k, v, gb, gdn_norm_g[l].reshape(1, HEAD_DIM),
          w_b_out[l].astype(BF16), w_o[l].astype(BF16))

        x = pl.pallas_call(
            functools.partial(_ffn_kernel, final_norm=(l == depth - 1)),
            grid=grid,
            in_specs=[_tile_spec(D, ts), _const_spec((1, D)), _const_spec((D, 2 * d_ff)),
                      _const_spec((CONV_K, 2 * d_ff)), _const_spec((d_ff, D)), _const_spec((1, D))],
            out_specs=_tile_spec(D, ts),
            out_shape=tok,
            scratch_shapes=[pltpu.VMEM((ts + CARRY_ROWS, 2 * d_ff), F32)],
            compiler_params=_params(),
            name="conv_ffn",
        )(x, norm_ffn_g[l].reshape(1, D), w_up[l].astype(BF16), ffn_conv_w[l],
          w_down[l].astype(BF16), norm_final_g.reshape(1, D))
    return x
```

```python
import functools

import jax
import jax.numpy as jnp
from jax import lax
from jax.experimental import pallas as pl
from jax.experimental.pallas import tpu as pltpu

D_MODEL = 1024
HEADS = 8
HEAD_DIM = 128
GDN_CONV_K = 4
CONV_K = 3
EPS = 1e-6
LANES = 128
CARRY_ROWS = 8
SEQ_TILE = 256
CHUNK = 64
VMEM_LIMIT = 52 * 1024 * 1024

F32 = jnp.float32
BF16 = jnp.bfloat16


def _dot(a, b):
    return jnp.dot(a, b, preferred_element_type=F32)


def _dot_nt(a, b):
    return lax.dot_general(a, b, (((1,), (1,)), ((), ())), preferred_element_type=F32)


def _dot_tn(a, b):
    return lax.dot_general(a, b, (((0,), (0,)), ((), ())), preferred_element_type=F32)


def _sigmoid(x):
    return 1.0 / (1.0 + jnp.exp(-x))


def _rmsnorm(x, g):
    return x * lax.rsqrt(jnp.mean(x * x, axis=-1, keepdims=True) + EPS) * g


def _causal_conv(buf, w, taps, rows):
    acc = None
    for j in range(taps):
        start = CARRY_ROWS - taps + 1 + j
        term = w[j:j + 1, :] * buf[start:start + rows, :]
        acc = term if acc is None else acc + term
    return acc


def _split3(x):
    hi = x.astype(BF16)
    r = x - hi.astype(F32)
    mid = r.astype(BF16)
    lo = (r - mid.astype(F32)).astype(BF16)
    return hi, mid, lo


def _in_proj_kernel(x_ref, ng_ref, wconv_ref, wqkv_ref, wz_ref, wab_ref, wgate_ref, waout_ref,
                    cwa_ref, cwg_ref, alog_ref, dtb_ref,
                    mixa_ref, sgb_ref, zs_ref, q_ref, k_ref, v_ref, gb_ref,
                    cbuf, qbuf):
    ts = x_ref.shape[1]

    @pl.when(pl.program_id(1) == 0)
    def _():
        cbuf[0:CARRY_ROWS, :] = jnp.zeros((CARRY_ROWS, cbuf.shape[1]), F32)
        qbuf[0:CARRY_ROWS, :] = jnp.zeros((CARRY_ROWS, qbuf.shape[1]), F32)

    h = _rmsnorm(x_ref[0], ng_ref[...]).astype(BF16)

    p = _dot(h, wconv_ref[...])
    cbuf[CARRY_ROWS:CARRY_ROWS + ts, :] = p[:, D_MODEL:2 * D_MODEL] * p[:, 2 * D_MODEL:]
    conv = _causal_conv(cbuf, cwa_ref[...], CONV_K, ts)
    cbuf[0:CARRY_ROWS, :] = cbuf[ts:ts + CARRY_ROWS, :]
    y_a = _dot((p[:, :D_MODEL] * conv).astype(BF16), waout_ref[...])
    pg = _dot(h, wgate_ref[...])
    mixa_ref[0] = _sigmoid(pg[:, :D_MODEL]) * y_a
    sgb_ref[0] = _sigmoid(pg[:, D_MODEL:])

    pz = _dot(h, wz_ref[...])
    zs_ref[0] = pz * _sigmoid(pz)

    pab = _dot(h, wab_ref[...])
    sp = pab + dtb_ref[...]
    softplus = jnp.maximum(sp, 0.0) + jnp.log1p(jnp.exp(-jnp.abs(sp)))
    lane = lax.broadcasted_iota(jnp.int32, pab.shape, 1)
    gb_ref[0] = jnp.where(lane < HEADS, -jnp.exp(alog_ref[...]) * softplus, _sigmoid(pab))

    qbuf[CARRY_ROWS:CARRY_ROWS + ts, :] = _dot(h, wqkv_ref[...])
    c = _causal_conv(qbuf, cwg_ref[...], GDN_CONV_K, ts)
    qbuf[0:CARRY_ROWS, :] = qbuf[ts:ts + CARRY_ROWS, :]
    c = c * _sigmoid(c)
    for hd in range(HEADS):
        lo = hd * HEAD_DIM
        qh = c[:, lo:lo + HEAD_DIM]
        kh = c[:, D_MODEL + lo:D_MODEL + lo + HEAD_DIM]
        q_ref[0, :, lo:lo + HEAD_DIM] = qh * (
            lax.rsqrt(jnp.sum(qh * qh, axis=-1, keepdims=True) + EPS) * (HEAD_DIM ** -0.5))
        k_ref[0, :, lo:lo + HEAD_DIM] = kh * lax.rsqrt(jnp.sum(kh * kh, axis=-1, keepdims=True) + EPS)
    v_ref[0] = c[:, 2 * D_MODEL:]


def _gdn_kernel(x_ref, mixa_ref, sgb_ref, zs_ref, q_ref, k_ref, v_ref, gb_ref, gng_ref, wb_ref, wo_ref,
                x1_ref, state, ogbuf):
    ts = x_ref.shape[1]

    @pl.when(pl.program_id(1) == 0)
    def _():
        state[...] = jnp.zeros(state.shape, F32)

    row = lax.broadcasted_iota(jnp.int32, (ts, ts), 0)
    col = lax.broadcasted_iota(jnp.int32, (ts, ts), 1)
    same = (row // CHUNK) == (col // CHUNK)
    tril = same & (col <= row)
    strict = same & (col < row)
    eye = (row == col).astype(F32)

    gbv = gb_ref[0]
    tril_b = tril.astype(BF16)
    same_b = same.astype(BF16)
    G = None
    GL = None
    for part in _split3(gbv):
        G = _dot(tril_b, part) if G is None else G + _dot(tril_b, part)
        GL = _dot(same_b, part) if GL is None else GL + _dot(same_b, part)
    GT = G.T
    eG = jnp.exp(G)
    eGLG = jnp.exp(GL - G)
    eGL = jnp.exp(GL)
    gng = gng_ref[...]

    heads = range(HEADS)
    lanes = [slice(hd * HEAD_DIM, (hd + 1) * HEAD_DIM) for hd in heads]

    Ls, As, qds, kds, rhss = [], [], [], [], []
    for hd in heads:
        q = q_ref[0, :, lanes[hd]]
        k = k_ref[0, :, lanes[hd]]
        v = v_ref[0, :, lanes[hd]]
        beta = gbv[:, HEADS + hd:HEADS + hd + 1]
        decay = jnp.where(tril, jnp.exp(G[:, hd:hd + 1] - GT[hd:hd + 1, :]), 0.0)
        eg = eG[:, hd:hd + 1]
        kb = k * beta
        k_b = k.astype(BF16)
        qds.append((q * eg).astype(BF16))
        kds.append((k * eGLG[:, hd:hd + 1]).astype(BF16))
        rhss.append(jnp.concatenate([v * beta, kb * eg], axis=1).astype(BF16))
        Ls.append(jnp.where(strict, _dot_nt(kb.astype(BF16), k_b) * decay, 0.0))
        As.append((_dot_nt(q.astype(BF16), k_b) * decay).astype(BF16))

    Ts = [eye - L for L in Ls]
    Ps = [L.astype(BF16) for L in Ls]
    n = 2
    while n < CHUNK:
        Ps = [_dot(P, P).astype(BF16) for P in Ps]
        Ts = [T + _dot(T.astype(BF16), P) for T, P in zip(Ts, Ps)]
        n *= 2
    uws = [_dot(T.astype(BF16), rhs) for T, rhs in zip(Ts, rhss)]
    us = [uw[:, :HEAD_DIM] for uw in uws]
    ws = [uw[:, HEAD_DIM:].astype(BF16) for uw in uws]

    Ss = [state[hd] for hd in heads]
    for c in range(ts // CHUNK):
        rows = slice(c * CHUNK, (c + 1) * CHUNK)
        wqs = [_dot(jnp.concatenate([ws[hd][rows], qds[hd][rows]], axis=0), Ss[hd].astype(BF16))
               for hd in heads]
        vns = [(us[hd][rows] - wqs[hd][:CHUNK]).astype(BF16) for hd in heads]
        Ss = [Ss[hd] * eGL[c * CHUNK:c * CHUNK + 1, hd:hd + 1] + _dot_tn(kds[hd][rows], vns[hd])
              for hd in heads]
        os_ = [wqs[hd][CHUNK:] + _dot(As[hd][rows, rows], vns[hd]) for hd in heads]
        for hd in heads:
            o = os_[hd]
            o = o * lax.rsqrt(jnp.mean(o * o, axis=-1, keepdims=True) + EPS) * gng
            ogbuf[rows, lanes[hd]] = (o * zs_ref[0, rows, lanes[hd]]).astype(BF16)
    for hd in heads:
        state[hd] = Ss[hd]

    y_b = _dot(ogbuf[...], wb_ref[...])
    mix = mixa_ref[0] + sgb_ref[0] * y_b
    x1_ref[0] = x_ref[0] + _dot(mix.astype(BF16), wo_ref[...])


def _ffn_kernel(x1_ref, ng_ref, wup_ref, cw_ref, wdown_ref, nf_ref, out_ref, ubuf, *, final_norm):
    ts = x1_ref.shape[1]
    d_ff = wdown_ref.shape[0]

    @pl.when(pl.program_id(1) == 0)
    def _():
        ubuf[0:CARRY_ROWS, :] = jnp.zeros((CARRY_ROWS, ubuf.shape[1]), F32)

    x1 = x1_ref[0]
    h = _rmsnorm(x1, ng_ref[...]).astype(BF16)
    ubuf[CARRY_ROWS:CARRY_ROWS + ts, :] = _dot(h, wup_ref[...])
    up = _causal_conv(ubuf, cw_ref[...], CONV_K, ts)
    ubuf[0:CARRY_ROWS, :] = ubuf[ts:ts + CARRY_ROWS, :]
    gate = up[:, :d_ff]
    act = (gate * _sigmoid(gate) * up[:, d_ff:]).astype(BF16)
    x2 = x1 + _dot(act, wdown_ref[...])
    out_ref[0] = _rmsnorm(x2, nf_ref[...]) if final_norm else x2


def _const_spec(shape):
    return pl.BlockSpec(shape, lambda b, s: (0,) * len(shape), pipeline_mode=pl.Buffered(1))


def _tile_spec(width, ts):
    return pl.BlockSpec((1, ts, width), lambda b, s: (b, s, 0))


def _params():
    return pltpu.CompilerParams(dimension_semantics=("arbitrary", "arbitrary"),
                                vmem_limit_bytes=VMEM_LIMIT)


def kernel(x, norm_mix_g, w_in, conv_a_w, gdn_conv_w, gdn_A_log, gdn_dt_bias, gdn_norm_g,
           w_a_out, w_b_out, w_o, norm_ffn_g, w_up, ffn_conv_w, w_down, norm_final_g):
    B, S, D = x.shape
    depth = w_in.shape[0]
    ts = SEQ_TILE
    grid = (B, S // ts)
    d_ff = w_down.shape[1]
    tok = jax.ShapeDtypeStruct((B, S, D), F32)

    for l in range(depth):
        wi = w_in[l].astype(BF16)
        w_conv = wi[:, :3 * D]
        w_qkv = wi[:, 3 * D:6 * D]
        w_z = wi[:, 6 * D:7 * D]
        w_ab = jnp.pad(wi[:, 7 * D:7 * D + 2 * HEADS], ((0, 0), (0, LANES - 2 * HEADS)))
        w_gate = wi[:, 7 * D + 2 * HEADS:]
        alog = jnp.pad(gdn_A_log[l], (0, LANES - HEADS)).reshape(1, LANES)
        dtb = jnp.pad(gdn_dt_bias[l], (0, LANES - HEADS)).reshape(1, LANES)

        mixa, sgb, zs, q, k, v, gb = pl.pallas_call(
            _in_proj_kernel,
            grid=grid,
            in_specs=[_tile_spec(D, ts), _const_spec((1, D)),
                      _const_spec((D, 3 * D)), _const_spec((D, 3 * D)), _const_spec((D, D)),
                      _const_spec((D, LANES)), _const_spec((D, 2 * D)), _const_spec((D, D)),
                      _const_spec((CONV_K, D)), _const_spec((GDN_CONV_K, 3 * D)),
                      _const_spec((1, LANES)), _const_spec((1, LANES))],
            out_specs=[_tile_spec(D, ts)] * 6 + [_tile_spec(LANES, ts)],
            out_shape=[tok] * 6 + [jax.ShapeDtypeStruct((B, S, LANES), F32)],
            scratch_shapes=[pltpu.VMEM((ts + CARRY_ROWS, D), F32),
                            pltpu.VMEM((ts + CARRY_ROWS, 3 * D), F32)],
            compiler_params=_params(),
            name="in_proj",
        )(x, norm_mix_g[l].reshape(1, D), w_conv, w_qkv, w_z, w_ab, w_gate, w_a_out[l].astype(BF16),
          conv_a_w[l], gdn_conv_w[l], alog, dtb)

        x = pl.pallas_call(
            _gdn_kernel,
            grid=grid,
            in_specs=[_tile_spec(D, ts)] * 7 + [_tile_spec(LANES, ts), _const_spec((1, HEAD_DIM)),
                                                 _const_spec((D, D)), _const_spec((D, D))],
            out_specs=_tile_spec(D, ts),
            out_shape=tok,
            scratch_shapes=[pltpu.VMEM((HEADS, HEAD_DIM, HEAD_DIM), F32),
                            pltpu.VMEM((ts, D), BF16)],
            compiler_params=_params(),
            name="gdn_mix",
        )(x, mixa, sgb, zs, q, k, v, gb, gdn_norm_g[l].reshape(1, HEAD_DIM),
          w_b_out[l].astype(BF16), w_o[l].astype(BF16))

        x = pl.pallas_call(
            functools.partial(_ffn_kernel, final_norm=(l == depth - 1)),
            grid=grid,
            in_specs=[_tile_spec(D, ts), _const_spec((1, D)), _const_spec((D, 2 * d_ff)),
                      _const_spec((CONV_K, 2 * d_ff)), _const_spec((d_ff, D)), _const_spec((1, D))],
            out_specs=_tile_spec(D, ts),
            out_shape=tok,
            scratch_shapes=[pltpu.VMEM((ts + CARRY_ROWS, 2 * d_ff), F32)],
            compiler_params=_params(),
            name="conv_ffn",
        )(x, norm_ffn_g[l].reshape(1, D), w_up[l].astype(BF16), ffn_conv_w[l],
          w_down[l].astype(BF16), norm_final_g.reshape(1, D))
    return x
```

```python
import functools

import jax
import jax.numpy as jnp
from jax import lax
from jax.experimental import pallas as pl
from jax.experimental.pallas import tpu as pltpu

D_MODEL = 1024
HEADS = 8
HEAD_DIM = 128
GDN_CONV_K = 4
CONV_K = 3
EPS = 1e-6
LANES = 128
CARRY_ROWS = 8
SEQ_TILE = 256
CHUNK = 64
VMEM_LIMIT = 52 * 1024 * 1024

F32 = jnp.float32
BF16 = jnp.bfloat16


def _dot(a, b):
    return jnp.dot(a, b, preferred_element_type=F32)


def _dot_nt(a, b):
    return lax.dot_general(a, b, (((1,), (1,)), ((), ())), preferred_element_type=F32)


def _dot_tn(a, b):
    return lax.dot_general(a, b, (((0,), (0,)), ((), ())), preferred_element_type=F32)


def _pack(w):
    k, n = w.shape
    pairs = w.astype(BF16).reshape(k // 2, 2, n).transpose(0, 2, 1)
    return lax.bitcast_convert_type(pairs, jnp.int32)


def _unpack(x_i32):
    return pltpu.bitcast(x_i32, BF16)


def _as_pairs(x):
    return pltpu.bitcast(x.astype(BF16), jnp.int32)


def _sigmoid(x):
    return 0.5 * jnp.tanh(0.5 * x) + 0.5


def _rmsnorm(x, g):
    return x * lax.rsqrt(jnp.mean(x * x, axis=-1, keepdims=True) + EPS) * g


def _causal_conv(x, carry_ref, w):
    taps = w.shape[0]
    rows = x.shape[0]
    carry = carry_ref[...]
    sub = lax.broadcasted_iota(jnp.int32, carry.shape, 0)
    acc = w[taps - 1:taps, :] * x
    for d in range(1, taps):
        rolled = pltpu.roll(x, d, 0)
        head = jnp.where(sub < d, pltpu.roll(carry, d, 0), rolled[:CARRY_ROWS])
        shifted = jnp.concatenate([head, rolled[CARRY_ROWS:]], axis=0)
        acc = acc + w[taps - 1 - d:taps - d, :] * shifted
    carry_ref[...] = x[rows - CARRY_ROWS:, :]
    return acc


def _split3(x):
    hi = x.astype(BF16)
    r = x - hi.astype(F32)
    mid = r.astype(BF16)
    lo = (r - mid.astype(F32)).astype(BF16)
    return hi, mid, lo


def _in_proj_kernel(x_ref, ng_ref, wconv_ref, wqkv_ref, wz_ref, wab_ref, wgate_ref, waout_ref,
                    cwa_ref, cwg_ref, alog_ref, dtb_ref,
                    mixa_ref, sgb_ref, zs_ref, q_ref, k_ref, v_ref, gb_ref,
                    cbuf, qbuf):
    @pl.when(pl.program_id(1) == 0)
    def _():
        cbuf[...] = jnp.zeros(cbuf.shape, F32)
        qbuf[...] = jnp.zeros(qbuf.shape, F32)

    h = _rmsnorm(x_ref[0], ng_ref[...]).astype(BF16)

    def conv_silu(group):
        cols = slice(group * D_MODEL, (group + 1) * D_MODEL)
        c = _causal_conv(_dot(h, _unpack(wqkv_ref[:, cols])), qbuf.at[:, cols], cwg_ref[:, cols])
        return c * _sigmoid(c)

    def l2norm_heads(c, scale, out_ref):
        for hd in range(HEADS):
            lanes = slice(hd * HEAD_DIM, (hd + 1) * HEAD_DIM)
            ch = c[:, lanes]
            inv = lax.rsqrt(jnp.sum(ch * ch, axis=-1, keepdims=True) + EPS)
            out_ref[0, :, lanes] = _as_pairs(ch * (inv * scale if scale != 1.0 else inv))

    l2norm_heads(conv_silu(0), HEAD_DIM ** -0.5, q_ref)
    l2norm_heads(conv_silu(1), 1.0, k_ref)
    v_ref[0] = _as_pairs(conv_silu(2))

    pz = _dot(h, _unpack(wz_ref[...]))
    zs_ref[0] = _as_pairs(pz * _sigmoid(pz))

    pab = _dot(h, _unpack(wab_ref[...]))
    sp = pab + dtb_ref[...]
    softplus = jnp.maximum(sp, 0.0) + jnp.log1p(jnp.exp(-jnp.abs(sp)))
    lane = lax.broadcasted_iota(jnp.int32, pab.shape, 1)
    gb_ref[0] = jnp.where(lane < HEADS, -jnp.exp(alog_ref[...]) * softplus, _sigmoid(pab))

    cgate = _dot(h, _unpack(wconv_ref[:, D_MODEL:2 * D_MODEL]))
    conv = _causal_conv(cgate * _dot(h, _unpack(wconv_ref[:, 2 * D_MODEL:])), cbuf, cwa_ref[...])
    y_a = _dot((_dot(h, _unpack(wconv_ref[:, :D_MODEL])) * conv).astype(BF16), _unpack(waout_ref[...]))
    mixa_ref[0] = _as_pairs(_sigmoid(_dot(h, _unpack(wgate_ref[:, :D_MODEL]))) * y_a)
    sgb_ref[0] = _as_pairs(_sigmoid(_dot(h, _unpack(wgate_ref[:, D_MODEL:]))))


def _gdn_kernel(x_ref, mixa_ref, sgb_ref, zs_ref, q_ref, k_ref, v_ref, gb_ref, gng_ref, wb_ref, wo_ref,
                x1_ref, state, ogbuf):
    ts = x_ref.shape[1]

    @pl.when(pl.program_id(1) == 0)
    def _():
        state[...] = jnp.zeros(state.shape, F32)

    row = lax.broadcasted_iota(jnp.int32, (ts, ts), 0)
    col = lax.broadcasted_iota(jnp.int32, (ts, ts), 1)
    same = (row // CHUNK) == (col // CHUNK)
    tril = same & (col <= row)
    strict = same & (col < row)
    eye = (row == col).astype(F32)

    gbv = gb_ref[0]
    tril_b = tril.astype(BF16)
    same_b = same.astype(BF16)
    G = None
    GL = None
    for part in _split3(gbv):
        G = _dot(tril_b, part) if G is None else G + _dot(tril_b, part)
        GL = _dot(same_b, part) if GL is None else GL + _dot(same_b, part)
    GT = G.T
    eG = jnp.exp(G)
    eGLG = jnp.exp(GL - G)
    eGL = jnp.exp(GL)
    gng = gng_ref[...]

    heads = range(HEADS)
    lanes = [slice(hd * HEAD_DIM, (hd + 1) * HEAD_DIM) for hd in heads]

    Ls, As, qds, kds, rhss = [], [], [], [], []
    for hd in heads:
        q = _unpack(q_ref[0, :, lanes[hd]])
        k = _unpack(k_ref[0, :, lanes[hd]])
        v = _unpack(v_ref[0, :, lanes[hd]])
        beta = gbv[:, HEADS + hd:HEADS + hd + 1]
        decay = jnp.where(tril, jnp.exp(G[:, hd:hd + 1] - GT[hd:hd + 1, :]), 0.0)
        eg = eG[:, hd:hd + 1]
        kf = k.astype(F32)
        kb = kf * beta
        qds.append((q.astype(F32) * eg).astype(BF16))
        kds.append((kf * eGLG[:, hd:hd + 1]).astype(BF16))
        rhss.append(jnp.concatenate([v.astype(F32) * beta, kb * eg], axis=1).astype(BF16))
        Ls.append(jnp.where(strict, _dot_nt(kb.astype(BF16), k) * decay, 0.0))
        As.append((_dot_nt(q, k) * decay).astype(BF16))

    Ts = [eye - L for L in Ls]
    Ps = [L.astype(BF16) for L in Ls]
    n = 2
    while n < CHUNK:
        Ps = [_dot(P, P).astype(BF16) for P in Ps]
        Ts = [T + _dot(T.astype(BF16), P) for T, P in zip(Ts, Ps)]
        n *= 2
    uws = [_dot(T.astype(BF16), rhs) for T, rhs in zip(Ts, rhss)]
    us = [uw[:, :HEAD_DIM] for uw in uws]
    ws = [uw[:, HEAD_DIM:].astype(BF16) for uw in uws]

    Ss = [state[hd] for hd in heads]
    for c in range(ts // CHUNK):
        rows = slice(c * CHUNK, (c + 1) * CHUNK)
        pair_rows = slice(c * CHUNK // 2, (c + 1) * CHUNK // 2)
        wqs = [_dot(jnp.concatenate([ws[hd][rows], qds[hd][rows]], axis=0), Ss[hd].astype(BF16))
               for hd in heads]
        vns = [(us[hd][rows] - wqs[hd][:CHUNK]).astype(BF16) for hd in heads]
        Ss = [Ss[hd] * eGL[c * CHUNK:c * CHUNK + 1, hd:hd + 1] + _dot_tn(kds[hd][rows], vns[hd])
              for hd in heads]
        os_ = [wqs[hd][CHUNK:] + _dot(As[hd][rows, rows], vns[hd]) for hd in heads]
        for hd in heads:
            o = os_[hd]
            o = o * lax.rsqrt(jnp.mean(o * o, axis=-1, keepdims=True) + EPS) * gng
            zs = _unpack(zs_ref[0, pair_rows, lanes[hd]]).astype(F32)
            ogbuf[rows, lanes[hd]] = (o * zs).astype(BF16)
    for hd in heads:
        state[hd] = Ss[hd]

    y_b = _dot(ogbuf[...], _unpack(wb_ref[...]))
    mix = _unpack(mixa_ref[0]).astype(F32) + _unpack(sgb_ref[0]).astype(F32) * y_b
    x1_ref[0] = x_ref[0] + _dot(mix.astype(BF16), _unpack(wo_ref[...]))


def _ffn_kernel(x1_ref, ng_ref, wup_ref, cw_ref, wdown_ref, nf_ref, out_ref, ubuf, *, final_norm):
    d_ff = 2 * wdown_ref.shape[0]

    @pl.when(pl.program_id(1) == 0)
    def _():
        ubuf[...] = jnp.zeros(ubuf.shape, F32)

    x1 = x1_ref[0]
    h = _rmsnorm(x1, ng_ref[...]).astype(BF16)
    up = _causal_conv(_dot(h, _unpack(wup_ref[...])), ubuf, cw_ref[...])
    gate = up[:, :d_ff]
    act = (gate * _sigmoid(gate) * up[:, d_ff:]).astype(BF16)
    x2 = x1 + _dot(act, _unpack(wdown_ref[...]))
    out_ref[0] = _rmsnorm(x2, nf_ref[...]) if final_norm else x2


def _const_spec(shape):
    return pl.BlockSpec(shape, lambda b, s: (0,) * len(shape), pipeline_mode=pl.Buffered(1))


def _tile_spec(width, rows):
    return pl.BlockSpec((1, rows, width), lambda b, s: (b, s, 0))


def _params():
    return pltpu.CompilerParams(dimension_semantics=("arbitrary", "arbitrary"),
                                vmem_limit_bytes=VMEM_LIMIT)


def kernel(x, norm_mix_g, w_in, conv_a_w, gdn_conv_w, gdn_A_log, gdn_dt_bias, gdn_norm_g,
           w_a_out, w_b_out, w_o, norm_ffn_g, w_up, ffn_conv_w, w_down, norm_final_g):
    B, S, D = x.shape
    depth = w_in.shape[0]
    ts = SEQ_TILE
    grid = (B, S // ts)
    d_ff = w_down.shape[1]
    tok = jax.ShapeDtypeStruct((B, S, D), F32)
    tok_pairs = jax.ShapeDtypeStruct((B, S // 2, D), jnp.int32)

    for l in range(depth):
        wi = w_in[l]
        w_conv = _pack(wi[:, :3 * D])
        w_qkv = _pack(wi[:, 3 * D:6 * D])
        w_z = _pack(wi[:, 6 * D:7 * D])
        w_ab = _pack(jnp.pad(wi[:, 7 * D:7 * D + 2 * HEADS], ((0, 0), (0, LANES - 2 * HEADS))))
        w_gate = _pack(wi[:, 7 * D + 2 * HEADS:])
        alog = jnp.pad(gdn_A_log[l], (0, LANES - HEADS)).reshape(1, LANES)
        dtb = jnp.pad(gdn_dt_bias[l], (0, LANES - HEADS)).reshape(1, LANES)

        mixa, sgb, zs, q, k, v, gb = pl.pallas_call(
            _in_proj_kernel,
            grid=grid,
            in_specs=[_tile_spec(D, ts), _const_spec((1, D)),
                      _const_spec((D // 2, 3 * D)), _const_spec((D // 2, 3 * D)), _const_spec((D // 2, D)),
                      _const_spec((D // 2, LANES)), _const_spec((D // 2, 2 * D)), _const_spec((D // 2, D)),
                      _const_spec((CONV_K, D)), _const_spec((GDN_CONV_K, 3 * D)),
                      _const_spec((1, LANES)), _const_spec((1, LANES))],
            out_specs=[_tile_spec(D, ts // 2)] * 6 + [_tile_spec(LANES, ts)],
            out_shape=[tok_pairs] * 6 + [jax.ShapeDtypeStruct((B, S, LANES), F32)],
            scratch_shapes=[pltpu.VMEM((CARRY_ROWS, D), F32),
                            pltpu.VMEM((CARRY_ROWS, 3 * D), F32)],
            compiler_params=_params(),
            name="in_proj",
        )(x, norm_mix_g[l].reshape(1, D), w_conv, w_qkv, w_z, w_ab, w_gate, _pack(w_a_out[l]),
          conv_a_w[l], gdn_conv_w[l], alog, dtb)

        x = pl.pallas_call(
            _gdn_kernel,
            grid=grid,
            in_specs=[_tile_spec(D, ts)] + [_tile_spec(D, ts // 2)] * 6 + [
                _tile_spec(LANES, ts), _const_spec((1, HEAD_DIM)),
                _const_spec((D // 2, D)), _const_spec((D // 2, D))],
            out_specs=_tile_spec(D, ts),
            out_shape=tok,
            scratch_shapes=[pltpu.VMEM((HEADS, HEAD_DIM, HEAD_DIM), F32),
                            pltpu.VMEM((ts, D), BF16)],
            compiler_params=_params(),
            name="gdn_mix",
        )(x, mixa, sgb, zs, q, k, v, gb, gdn_norm_g[l].reshape(1, HEAD_DIM),
          _pack(w_b_out[l]), _pack(w_o[l]))

        x = pl.pallas_call(
            functools.partial(_ffn_kernel, final_norm=(l == depth - 1)),
            grid=grid,
            in_specs=[_tile_spec(D, ts), _const_spec((1, D)), _const_spec((D // 2, 2 * d_ff)),
                      _const_spec((CONV_K, 2 * d_ff)), _const_spec((d_ff // 2, D)), _const_spec((1, D))],
            out_specs=_tile_spec(D, ts),
            out_shape=tok,
            scratch_shapes=[pltpu.VMEM((CARRY_ROWS, 2 * d_ff), F32)],
            compiler_params=_params(),
            name="conv_ffn",
        )(x, norm_ffn_g[l].reshape(1, D), _pack(w_up[l]), ffn_conv_w[l],
          _pack(w_down[l]), norm_final_g.reshape(1, D))
    return x
```

```python
import functools

import jax
import jax.numpy as jnp
from jax import lax
from jax.experimental import pallas as pl
from jax.experimental.pallas import tpu as pltpu

D_MODEL = 1024
HEADS = 8
HEAD_DIM = 128
GDN_CONV_K = 4
CONV_K = 3
EPS = 1e-6
LANES = 128
CARRY_ROWS = 8
SEQ_TILE = 256
CHUNK = 64
VMEM_LIMIT = 52 * 1024 * 1024

F32 = jnp.float32
BF16 = jnp.bfloat16


def _dot(a, b):
    return jnp.dot(a, b, preferred_element_type=F32)


def _dot_nt(a, b):
    return lax.dot_general(a, b, (((1,), (1,)), ((), ())), preferred_element_type=F32)


def _dot_tn(a, b):
    return lax.dot_general(a, b, (((0,), (0,)), ((), ())), preferred_element_type=F32)


def _pack_kernel(w_ref, o_ref):
    o_ref[...] = _as_pairs(w_ref[...])


def _pack(w, block_rows=None, block_cols=None, n_col_blocks=None):
    k, n = w.shape
    bk = block_rows or k
    bn = block_cols or n
    ncb = n_col_blocks or n // bn
    return pl.pallas_call(
        _pack_kernel,
        grid=(k // bk, ncb),
        in_specs=[pl.BlockSpec((bk, bn), lambda i, j: (i, j))],
        out_specs=pl.BlockSpec((bk // 2, bn), lambda i, j: (i, j)),
        out_shape=jax.ShapeDtypeStruct((k // 2, ncb * bn), jnp.int32),
        compiler_params=pltpu.CompilerParams(dimension_semantics=("arbitrary", "arbitrary")),
        name="pack_weight",
    )(w)


def _unpack(x_i32):
    return pltpu.bitcast(x_i32, BF16)


def _as_pairs(x):
    return pltpu.bitcast(x.astype(BF16), jnp.int32)


def _sigmoid(x):
    return 0.5 * jnp.tanh(0.5 * x) + 0.5


def _rmsnorm(x, g):
    return x * lax.rsqrt(jnp.mean(x * x, axis=-1, keepdims=True) + EPS) * g


def _causal_conv(x, carry_ref, w):
    taps = w.shape[0]
    rows = x.shape[0]
    carry = carry_ref[...]
    sub = lax.broadcasted_iota(jnp.int32, carry.shape, 0)
    acc = w[taps - 1:taps, :] * x
    for d in range(1, taps):
        rolled = pltpu.roll(x, d, 0)
        head = jnp.where(sub < d, pltpu.roll(carry, d, 0), rolled[:CARRY_ROWS])
        shifted = jnp.concatenate([head, rolled[CARRY_ROWS:]], axis=0)
        acc = acc + w[taps - 1 - d:taps - d, :] * shifted
    carry_ref[...] = x[rows - CARRY_ROWS:, :]
    return acc


def _split3(x):
    hi = x.astype(BF16)
    r = x - hi.astype(F32)
    mid = r.astype(BF16)
    lo = (r - mid.astype(F32)).astype(BF16)
    return hi, mid, lo


def _in_proj_kernel(x_ref, ng_ref, wmain_ref, wab_ref, wgate_ref, waout_ref,
                    cwa_ref, cwg_ref, alog_ref, dtb_ref,
                    mixa_ref, sgb_ref, zs_ref, q_ref, k_ref, v_ref, gb_ref,
                    cbuf, qbuf):
    @pl.when(pl.program_id(1) == 0)
    def _():
        cbuf[...] = jnp.zeros(cbuf.shape, F32)
        qbuf[...] = jnp.zeros(qbuf.shape, F32)

    h = _rmsnorm(x_ref[0], ng_ref[...]).astype(BF16)

    def main_cols(block):
        return _unpack(wmain_ref[:, block * D_MODEL:(block + 1) * D_MODEL])

    def conv_silu(group):
        cols = slice(group * D_MODEL, (group + 1) * D_MODEL)
        c = _causal_conv(_dot(h, main_cols(3 + group)), qbuf.at[:, cols], cwg_ref[:, cols])
        return c * _sigmoid(c)

    def l2norm_heads(c, scale, out_ref):
        for hd in range(HEADS):
            lanes = slice(hd * HEAD_DIM, (hd + 1) * HEAD_DIM)
            ch = c[:, lanes]
            inv = lax.rsqrt(jnp.sum(ch * ch, axis=-1, keepdims=True) + EPS)
            out_ref[0, :, lanes] = _as_pairs(ch * (inv * scale if scale != 1.0 else inv))

    l2norm_heads(conv_silu(0), HEAD_DIM ** -0.5, q_ref)
    l2norm_heads(conv_silu(1), 1.0, k_ref)
    v_ref[0] = _as_pairs(conv_silu(2))

    pz = _dot(h, main_cols(6))
    zs_ref[0] = _as_pairs(pz * _sigmoid(pz))

    pab = _dot(h, _unpack(wab_ref[...]))
    sp = pab + dtb_ref[...]
    softplus = jnp.maximum(sp, 0.0) + jnp.log1p(jnp.exp(-jnp.abs(sp)))
    lane = lax.broadcasted_iota(jnp.int32, pab.shape, 1)
    gb_ref[0] = jnp.where(lane < HEADS, -jnp.exp(alog_ref[...]) * softplus, _sigmoid(pab))

    cgate = _dot(h, main_cols(1))
    conv = _causal_conv(cgate * _dot(h, main_cols(2)), cbuf, cwa_ref[...])
    y_a = _dot((_dot(h, main_cols(0)) * conv).astype(BF16), _unpack(waout_ref[...]))
    mixa_ref[0] = _as_pairs(_sigmoid(_dot(h, _unpack(wgate_ref[:, :D_MODEL]))) * y_a)
    sgb_ref[0] = _as_pairs(_sigmoid(_dot(h, _unpack(wgate_ref[:, D_MODEL:]))))


def _gdn_kernel(x_ref, mixa_ref, sgb_ref, zs_ref, q_ref, k_ref, v_ref, gb_ref, gng_ref, wb_ref, wo_ref,
                x1_ref, state, ogbuf):
    ts = x_ref.shape[1]

    @pl.when(pl.program_id(1) == 0)
    def _():
        state[...] = jnp.zeros(state.shape, F32)

    row = lax.broadcasted_iota(jnp.int32, (ts, ts), 0)
    col = lax.broadcasted_iota(jnp.int32, (ts, ts), 1)
    same = (row // CHUNK) == (col // CHUNK)
    tril = same & (col <= row)
    strict = same & (col < row)
    eye = (row == col).astype(F32)

    gbv = gb_ref[0]
    tril_b = tril.astype(BF16)
    same_b = same.astype(BF16)
    G = None
    GL = None
    for part in _split3(gbv):
        G = _dot(tril_b, part) if G is None else G + _dot(tril_b, part)
        GL = _dot(same_b, part) if GL is None else GL + _dot(same_b, part)
    GT = G.T
    eG = jnp.exp(G)
    eGLG = jnp.exp(GL - G)
    eGL = jnp.exp(GL)
    gng = gng_ref[...]

    heads = range(HEADS)
    lanes = [slice(hd * HEAD_DIM, (hd + 1) * HEAD_DIM) for hd in heads]

    Ls, As, qds, kds, rhss = [], [], [], [], []
    for hd in heads:
        q = _unpack(q_ref[0, :, lanes[hd]])
        k = _unpack(k_ref[0, :, lanes[hd]])
        v = _unpack(v_ref[0, :, lanes[hd]])
        beta = gbv[:, HEADS + hd:HEADS + hd + 1]
        decay = jnp.where(tril, jnp.exp(G[:, hd:hd + 1] - GT[hd:hd + 1, :]), 0.0)
        eg = eG[:, hd:hd + 1]
        kf = k.astype(F32)
        kb = kf * beta
        qds.append((q.astype(F32) * eg).astype(BF16))
        kds.append((kf * eGLG[:, hd:hd + 1]).astype(BF16))
        rhss.append(jnp.concatenate([v.astype(F32) * beta, kb * eg], axis=1).astype(BF16))
        Ls.append(jnp.where(strict, _dot_nt(kb.astype(BF16), k) * decay, 0.0))
        As.append((_dot_nt(q, k) * decay).astype(BF16))

    Ts = [eye - L for L in Ls]
    Ps = [L.astype(BF16) for L in Ls]
    n = 2
    while n < CHUNK:
        Ps = [_dot(P, P).astype(BF16) for P in Ps]
        Ts = [T + _dot(T.astype(BF16), P) for T, P in zip(Ts, Ps)]
        n *= 2
    uws = [_dot(T.astype(BF16), rhs) for T, rhs in zip(Ts, rhss)]
    us = [uw[:, :HEAD_DIM] for uw in uws]
    ws = [uw[:, HEAD_DIM:].astype(BF16) for uw in uws]

    Ss = [state[hd] for hd in heads]
    for c in range(ts // CHUNK):
        rows = slice(c * CHUNK, (c + 1) * CHUNK)
        pair_rows = slice(c * CHUNK // 2, (c + 1) * CHUNK // 2)
        wqs = [_dot(jnp.concatenate([ws[hd][rows], qds[hd][rows]], axis=0), Ss[hd].astype(BF16))
               for hd in heads]
        vns = [(us[hd][rows] - wqs[hd][:CHUNK]).astype(BF16) for hd in heads]
        Ss = [Ss[hd] * eGL[c * CHUNK:c * CHUNK + 1, hd:hd + 1] + _dot_tn(kds[hd][rows], vns[hd])
              for hd in heads]
        os_ = [wqs[hd][CHUNK:] + _dot(As[hd][rows, rows], vns[hd]) for hd in heads]
        for hd in heads:
            o = os_[hd]
            o = o * lax.rsqrt(jnp.mean(o * o, axis=-1, keepdims=True) + EPS) * gng
            zs = _unpack(zs_ref[0, pair_rows, lanes[hd]]).astype(F32)
            ogbuf[rows, lanes[hd]] = (o * zs).astype(BF16)
    for hd in heads:
        state[hd] = Ss[hd]

    y_b = _dot(ogbuf[...], _unpack(wb_ref[...]))
    mix = _unpack(mixa_ref[0]).astype(F32) + _unpack(sgb_ref[0]).astype(F32) * y_b
    x1_ref[0] = x_ref[0] + _dot(mix.astype(BF16), _unpack(wo_ref[...]))


def _ffn_kernel(x1_ref, ng_ref, wup_ref, cw_ref, wdown_ref, nf_ref, out_ref, ubuf, *, final_norm):
    d_ff = 2 * wdown_ref.shape[0]

    @pl.when(pl.program_id(1) == 0)
    def _():
        ubuf[...] = jnp.zeros(ubuf.shape, F32)

    x1 = x1_ref[0]
    h = _rmsnorm(x1, ng_ref[...]).astype(BF16)
    up = _causal_conv(_dot(h, _unpack(wup_ref[...])), ubuf, cw_ref[...])
    gate = up[:, :d_ff]
    act = (gate * _sigmoid(gate) * up[:, d_ff:]).astype(BF16)
    x2 = x1 + _dot(act, _unpack(wdown_ref[...]))
    out_ref[0] = _rmsnorm(x2, nf_ref[...]) if final_norm else x2


def _const_spec(shape):
    return pl.BlockSpec(shape, lambda b, s: (0,) * len(shape), pipeline_mode=pl.Buffered(1))


def _tile_spec(width, rows):
    return pl.BlockSpec((1, rows, width), lambda b, s: (b, s, 0))


def _params():
    return pltpu.CompilerParams(dimension_semantics=("arbitrary", "arbitrary"),
                                vmem_limit_bytes=VMEM_LIMIT)


def kernel(x, norm_mix_g, w_in, conv_a_w, gdn_conv_w, gdn_A_log, gdn_dt_bias, gdn_norm_g,
           w_a_out, w_b_out, w_o, norm_ffn_g, w_up, ffn_conv_w, w_down, norm_final_g):
    B, S, D = x.shape
    depth = w_in.shape[0]
    ts = SEQ_TILE
    grid = (B, S // ts)
    d_ff = w_down.shape[1]
    tok = jax.ShapeDtypeStruct((B, S, D), F32)
    tok_pairs = jax.ShapeDtypeStruct((B, S // 2, D), jnp.int32)

    for l in range(depth):
        wi = w_in[l]
        w_main = _pack(wi, block_cols=D, n_col_blocks=7)
        w_ab = _pack(jnp.pad(wi[:, 7 * D:7 * D + 2 * HEADS], ((0, 0), (0, LANES - 2 * HEADS))))
        w_gate = _pack(wi[:, 7 * D + 2 * HEADS:], block_cols=D)
        alog = jnp.pad(gdn_A_log[l], (0, LANES - HEADS)).reshape(1, LANES)
        dtb = jnp.pad(gdn_dt_bias[l], (0, LANES - HEADS)).reshape(1, LANES)

        mixa, sgb, zs, q, k, v, gb = pl.pallas_call(
            _in_proj_kernel,
            grid=grid,
            in_specs=[_tile_spec(D, ts), _const_spec((1, D)),
                      _const_spec((D // 2, 7 * D)),
                      _const_spec((D // 2, LANES)), _const_spec((D // 2, 2 * D)), _const_spec((D // 2, D)),
                      _const_spec((CONV_K, D)), _const_spec((GDN_CONV_K, 3 * D)),
                      _const_spec((1, LANES)), _const_spec((1, LANES))],
            out_specs=[_tile_spec(D, ts // 2)] * 6 + [_tile_spec(LANES, ts)],
            out_shape=[tok_pairs] * 6 + [jax.ShapeDtypeStruct((B, S, LANES), F32)],
            scratch_shapes=[pltpu.VMEM((CARRY_ROWS, D), F32),
                            pltpu.VMEM((CARRY_ROWS, 3 * D), F32)],
            compiler_params=_params(),
            name="in_proj",
        )(x, norm_mix_g[l].reshape(1, D), w_main, w_ab, w_gate, _pack(w_a_out[l]),
          conv_a_w[l], gdn_conv_w[l], alog, dtb)

        x = pl.pallas_call(
            _gdn_kernel,
            grid=grid,
            in_specs=[_tile_spec(D, ts)] + [_tile_spec(D, ts // 2)] * 6 + [
                _tile_spec(LANES, ts), _const_spec((1, HEAD_DIM)),
                _const_spec((D // 2, D)), _const_spec((D // 2, D))],
            out_specs=_tile_spec(D, ts),
            out_shape=tok,
            scratch_shapes=[pltpu.VMEM((HEADS, HEAD_DIM, HEAD_DIM), F32),
                            pltpu.VMEM((ts, D), BF16)],
            compiler_params=_params(),
            name="gdn_mix",
        )(x, mixa, sgb, zs, q, k, v, gb, gdn_norm_g[l].reshape(1, HEAD_DIM),
          _pack(w_b_out[l]), _pack(w_o[l]))

        x = pl.pallas_call(
            functools.partial(_ffn_kernel, final_norm=(l == depth - 1)),
            grid=grid,
            in_specs=[_tile_spec(D, ts), _const_spec((1, D)), _const_spec((D // 2, 2 * d_ff)),
                      _const_spec((CONV_K, 2 * d_ff)), _const_spec((d_ff // 2, D)), _const_spec((1, D))],
            out_specs=_tile_spec(D, ts),
            out_shape=tok,
            scratch_shapes=[pltpu.VMEM((CARRY_ROWS, 2 * d_ff), F32)],
            compiler_params=_params(),
            name="conv_ffn",
        )(x, norm_ffn_g[l].reshape(1, D), _pack(w_up[l], block_cols=512), ffn_conv_w[l],
          _pack(w_down[l], block_rows=704), norm_final_g.reshape(1, D))
    return x
```

```python
import functools

import jax
import jax.numpy as jnp
from jax import lax
from jax.experimental import pallas as pl
from jax.experimental.pallas import tpu as pltpu

D_MODEL = 1024
HEADS = 8
HEAD_DIM = 128
GDN_CONV_K = 4
CONV_K = 3
EPS = 1e-6
LANES = 128
CARRY_ROWS = 8
SEQ_TILE = 256
CHUNK = 64
COL_CHUNK = 256
VMEM_LIMIT = 52 * 1024 * 1024

F32 = jnp.float32
BF16 = jnp.bfloat16


def _dot(a, b):
    return jnp.dot(a, b, preferred_element_type=F32)


def _dot_nt(a, b):
    return lax.dot_general(a, b, (((1,), (1,)), ((), ())), preferred_element_type=F32)


def _dot_tn(a, b):
    return lax.dot_general(a, b, (((0,), (0,)), ((), ())), preferred_element_type=F32)


def _pack_kernel(w_ref, o_ref):
    o_ref[...] = _as_pairs(w_ref[...])


def _pack(w, layer=None, block_rows=None, block_cols=None, n_col_blocks=None):
    k, n = w.shape[-2:]
    bk = block_rows or k
    bn = block_cols or n
    ncb = n_col_blocks or n // bn
    if layer is None:
        in_spec = pl.BlockSpec((bk, bn), lambda i, j: (i, j))
    else:
        in_spec = pl.BlockSpec((None, bk, bn), lambda i, j: (layer, i, j))
    return pl.pallas_call(
        _pack_kernel,
        grid=(k // bk, ncb),
        in_specs=[in_spec],
        out_specs=pl.BlockSpec((bk // 2, bn), lambda i, j: (i, j)),
        out_shape=jax.ShapeDtypeStruct((k // 2, ncb * bn), jnp.int32),
        compiler_params=pltpu.CompilerParams(dimension_semantics=("arbitrary", "arbitrary")),
        name="pack_weight",
    )(w)


def _unpack(x_i32):
    return pltpu.bitcast(x_i32, BF16)


def _as_pairs(x):
    return pltpu.bitcast(x.astype(BF16), jnp.int32)


def _sigmoid(x):
    return 0.5 * jnp.tanh(0.5 * x) + 0.5


def _rmsnorm(x, g):
    return x * lax.rsqrt(jnp.mean(x * x, axis=-1, keepdims=True) + EPS) * g


def _causal_conv(x, carry_ref, w):
    taps = w.shape[0]
    rows = x.shape[0]
    carry = carry_ref[...]
    sub = lax.broadcasted_iota(jnp.int32, carry.shape, 0)
    acc = w[taps - 1:taps, :] * x
    for d in range(1, taps):
        rolled = pltpu.roll(x, d, 0)
        head = jnp.where(sub < d, pltpu.roll(carry, d, 0), rolled[:CARRY_ROWS])
        shifted = jnp.concatenate([head, rolled[CARRY_ROWS:]], axis=0)
        acc = acc + w[taps - 1 - d:taps - d, :] * shifted
    carry_ref[...] = x[rows - CARRY_ROWS:, :]
    return acc


def _split3(x):
    hi = x.astype(BF16)
    r = x - hi.astype(F32)
    mid = r.astype(BF16)
    lo = (r - mid.astype(F32)).astype(BF16)
    return hi, mid, lo


def _in_proj_kernel(x_ref, ng_ref, wmain_ref, wab_ref, wgate_ref, waout_ref,
                    cwa_ref, cwg_ref, alog_ref, dtb_ref,
                    mixa_ref, sgb_ref, zs_ref, q_ref, k_ref, v_ref, gb_ref,
                    cbuf, qbuf, yabuf, sgabuf):
    @pl.when(pl.program_id(1) == 0)
    def _():
        cbuf[...] = jnp.zeros(cbuf.shape, F32)
        qbuf[...] = jnp.zeros(qbuf.shape, F32)

    h = _rmsnorm(x_ref[0], ng_ref[...]).astype(BF16)

    def proj(block, cols):
        start = block * D_MODEL + cols.start
        return _dot(h, _unpack(wmain_ref[:, start:start + COL_CHUNK]))

    def conv_silu(group, cols):
        gcols = slice(group * D_MODEL + cols.start, group * D_MODEL + cols.stop)
        c = _causal_conv(proj(3 + group, cols), qbuf.at[:, gcols], cwg_ref[:, gcols])
        return c * _sigmoid(c)

    def l2norm_heads(c, scale, out_ref, cols):
        for lo in range(0, COL_CHUNK, HEAD_DIM):
            ch = c[:, lo:lo + HEAD_DIM]
            inv = lax.rsqrt(jnp.sum(ch * ch, axis=-1, keepdims=True) + EPS)
            out_ref[0, :, cols.start + lo:cols.start + lo + HEAD_DIM] = _as_pairs(
                ch * (inv * scale if scale != 1.0 else inv))

    for start in range(0, D_MODEL, COL_CHUNK):
        cols = slice(start, start + COL_CHUNK)
        l2norm_heads(conv_silu(0, cols), HEAD_DIM ** -0.5, q_ref, cols)
        conv = _causal_conv(proj(1, cols) * proj(2, cols), cbuf.at[:, cols], cwa_ref[:, cols])
        l2norm_heads(conv_silu(1, cols), 1.0, k_ref, cols)
        yabuf[:, cols] = (proj(0, cols) * conv).astype(BF16)
        sgabuf[:, cols] = _sigmoid(_dot(h, _unpack(wgate_ref[:, cols])))
        v_ref[0, :, cols] = _as_pairs(conv_silu(2, cols))
        sgb_ref[0, :, cols] = _as_pairs(_sigmoid(
            _dot(h, _unpack(wgate_ref[:, D_MODEL + start:D_MODEL + start + COL_CHUNK]))))
        pz = proj(6, cols)
        zs_ref[0, :, cols] = _as_pairs(pz * _sigmoid(pz))

    pab = _dot(h, _unpack(wab_ref[...]))
    sp = pab + dtb_ref[...]
    softplus = jnp.maximum(sp, 0.0) + jnp.log1p(jnp.exp(-jnp.abs(sp)))
    lane = lax.broadcasted_iota(jnp.int32, pab.shape, 1)
    gb_ref[0] = jnp.where(lane < HEADS, -jnp.exp(alog_ref[...]) * softplus, _sigmoid(pab))

    mixa_ref[0] = _as_pairs(sgabuf[...] * _dot(yabuf[...], _unpack(waout_ref[...])))


def _gdn_kernel(x_ref, mixa_ref, sgb_ref, zs_ref, q_ref, k_ref, v_ref, gb_ref, gng_ref, wb_ref, wo_ref,
                x1_ref, state, ogbuf):
    ts = x_ref.shape[1]

    @pl.when(pl.program_id(1) == 0)
    def _():
        state[...] = jnp.zeros(state.shape, F32)

    row = lax.broadcasted_iota(jnp.int32, (ts, ts), 0)
    col = lax.broadcasted_iota(jnp.int32, (ts, ts), 1)
    same = (row // CHUNK) == (col // CHUNK)
    tril = same & (col <= row)
    strict = same & (col < row)

    gbv = gb_ref[0]
    tril_b = tril.astype(BF16)
    same_b = same.astype(BF16)
    G = None
    GL = None
    for part in _split3(gbv):
        G = _dot(tril_b, part) if G is None else G + _dot(tril_b, part)
        GL = _dot(same_b, part) if GL is None else GL + _dot(same_b, part)
    GT = G.T
    eG = jnp.exp(G)
    eGLG = jnp.exp(GL - G)
    eGL = jnp.exp(GL)
    gng = gng_ref[...]

    heads = range(HEADS)
    lanes = [slice(hd * HEAD_DIM, (hd + 1) * HEAD_DIM) for hd in heads]

    Ls, As, qds, kds, rhss = [], [], [], [], []
    for hd in heads:
        q = _unpack(q_ref[0, :, lanes[hd]])
        k = _unpack(k_ref[0, :, lanes[hd]])
        v = _unpack(v_ref[0, :, lanes[hd]])
        beta = gbv[:, HEADS + hd:HEADS + hd + 1]
        decay = jnp.where(tril, jnp.exp(G[:, hd:hd + 1] - GT[hd:hd + 1, :]), 0.0)
        eg = eG[:, hd:hd + 1]
        kf = k.astype(F32)
        kb = kf * beta
        qds.append((q.astype(F32) * eg).astype(BF16))
        kds.append((kf * eGLG[:, hd:hd + 1]).astype(BF16))
        rhss.append(jnp.concatenate([v.astype(F32) * beta, kb * eg], axis=1).astype(BF16))
        Ls.append(jnp.where(strict, _dot_nt(kb.astype(BF16), k) * decay, 0.0))
        As.append((_dot_nt(q, k) * decay).astype(BF16))

    n_chunks = ts // CHUNK
    lane_chunk = lax.broadcasted_iota(jnp.int32, (CHUNK, ts), 1) // CHUNK
    eye_sbs = (lax.broadcasted_iota(jnp.int32, (CHUNK, ts), 0)
               == lax.broadcasted_iota(jnp.int32, (CHUNK, ts), 1) % CHUNK).astype(F32)

    def fold(m):
        acc = m[:CHUNK]
        for c in range(1, n_chunks):
            acc = acc + m[c * CHUNK:(c + 1) * CHUNK]
        return acc

    def unfold(m):
        return jnp.concatenate(
            [jnp.where(lane_chunk == c, m, jnp.zeros_like(m)) for c in range(n_chunks)], axis=0)

    L_sbs = [fold(L) for L in Ls]
    Ts = [eye_sbs - L for L in L_sbs]
    Ps = [_dot(Ls_.astype(BF16), L.astype(BF16)) for Ls_, L in zip(L_sbs, Ls)]
    n = 4
    while n < CHUNK:
        both = [_dot(jnp.concatenate([T.astype(BF16), P.astype(BF16)], axis=0), unfold(P.astype(BF16)))
                for T, P in zip(Ts, Ps)]
        Ts = [T + tp[:CHUNK] for T, tp in zip(Ts, both)]
        Ps = [tp[CHUNK:] for tp in both]
        n *= 2
    Ts = [T + _dot(T.astype(BF16), unfold(P.astype(BF16))) for T, P in zip(Ts, Ps)]
    uws = [_dot(unfold(T.astype(BF16)), rhs) for T, rhs in zip(Ts, rhss)]
    us = [uw[:, :HEAD_DIM] for uw in uws]
    ws = [uw[:, HEAD_DIM:].astype(BF16) for uw in uws]

    Ss = [state[hd] for hd in heads]
    for c in range(ts // CHUNK):
        rows = slice(c * CHUNK, (c + 1) * CHUNK)
        pair_rows = slice(c * CHUNK // 2, (c + 1) * CHUNK // 2)
        wqs = [_dot(jnp.concatenate([ws[hd][rows], qds[hd][rows]], axis=0), Ss[hd].astype(BF16))
               for hd in heads]
        vns = [(us[hd][rows] - wqs[hd][:CHUNK]).astype(BF16) for hd in heads]
        Ss = [Ss[hd] * eGL[c * CHUNK:c * CHUNK + 1, hd:hd + 1] + _dot_tn(kds[hd][rows], vns[hd])
              for hd in heads]
        os_ = [wqs[hd][CHUNK:] + _dot(As[hd][rows, rows], vns[hd]) for hd in heads]
        for hd in heads:
            o = os_[hd]
            o = o * lax.rsqrt(jnp.mean(o * o, axis=-1, keepdims=True) + EPS) * gng
            zs = _unpack(zs_ref[0, pair_rows, lanes[hd]]).astype(F32)
            ogbuf[rows, lanes[hd]] = (o * zs).astype(BF16)
    for hd in heads:
        state[hd] = Ss[hd]

    y_b = _dot(ogbuf[...], _unpack(wb_ref[...]))
    mix = _unpack(mixa_ref[0]).astype(F32) + _unpack(sgb_ref[0]).astype(F32) * y_b
    x1_ref[0] = x_ref[0] + _dot(mix.astype(BF16), _unpack(wo_ref[...]))


def _ffn_kernel(x1_ref, ng_ref, wup_ref, cw_ref, wdown_ref, nf_ref, out_ref, ubuf, *, final_norm):
    d_ff = 2 * wdown_ref.shape[0]

    @pl.when(pl.program_id(1) == 0)
    def _():
        ubuf[...] = jnp.zeros(ubuf.shape, F32)

    x1 = x1_ref[0]
    h = _rmsnorm(x1, ng_ref[...]).astype(BF16)
    up = _causal_conv(_dot(h, _unpack(wup_ref[...])), ubuf, cw_ref[...])
    gate = up[:, :d_ff]
    act = (gate * _sigmoid(gate) * up[:, d_ff:]).astype(BF16)
    x2 = x1 + _dot(act, _unpack(wdown_ref[...]))
    out_ref[0] = _rmsnorm(x2, nf_ref[...]) if final_norm else x2


def _const_spec(shape):
    return pl.BlockSpec(shape, lambda b, s: (0,) * len(shape), pipeline_mode=pl.Buffered(1))


def _tile_spec(width, rows):
    return pl.BlockSpec((1, rows, width), lambda b, s: (b, s, 0))


def _params():
    return pltpu.CompilerParams(dimension_semantics=("arbitrary", "arbitrary"),
                                vmem_limit_bytes=VMEM_LIMIT)


def kernel(x, norm_mix_g, w_in, conv_a_w, gdn_conv_w, gdn_A_log, gdn_dt_bias, gdn_norm_g,
           w_a_out, w_b_out, w_o, norm_ffn_g, w_up, ffn_conv_w, w_down, norm_final_g):
    B, S, D = x.shape
    depth = w_in.shape[0]
    ts = SEQ_TILE
    grid = (B, S // ts)
    d_ff = w_down.shape[1]
    tok = jax.ShapeDtypeStruct((B, S, D), F32)
    tok_pairs = jax.ShapeDtypeStruct((B, S // 2, D), jnp.int32)

    for l in range(depth):
        w_main = _pack(w_in, l, block_cols=D, n_col_blocks=7)
        w_ab = _pack(jnp.pad(w_in[l, :, 7 * D:7 * D + 2 * HEADS], ((0, 0), (0, LANES - 2 * HEADS))))
        w_gate = _pack(w_in[l, :, 7 * D + 2 * HEADS:], block_cols=D)
        alog = jnp.pad(gdn_A_log[l], (0, LANES - HEADS)).reshape(1, LANES)
        dtb = jnp.pad(gdn_dt_bias[l], (0, LANES - HEADS)).reshape(1, LANES)

        mixa, sgb, zs, q, k, v, gb = pl.pallas_call(
            _in_proj_kernel,
            grid=grid,
            in_specs=[_tile_spec(D, ts), _const_spec((1, D)),
                      _const_spec((D // 2, 7 * D)),
                      _const_spec((D // 2, LANES)), _const_spec((D // 2, 2 * D)), _const_spec((D // 2, D)),
                      _const_spec((CONV_K, D)), _const_spec((GDN_CONV_K, 3 * D)),
                      _const_spec((1, LANES)), _const_spec((1, LANES))],
            out_specs=[_tile_spec(D, ts // 2)] * 6 + [_tile_spec(LANES, ts)],
            out_shape=[tok_pairs] * 6 + [jax.ShapeDtypeStruct((B, S, LANES), F32)],
            scratch_shapes=[pltpu.VMEM((CARRY_ROWS, D), F32),
                            pltpu.VMEM((CARRY_ROWS, 3 * D), F32),
                            pltpu.VMEM((ts, D), BF16), pltpu.VMEM((ts, D), F32)],
            compiler_params=_params(),
            name="in_proj",
        )(x, norm_mix_g[l].reshape(1, D), w_main, w_ab, w_gate, _pack(w_a_out, l),
          conv_a_w[l], gdn_conv_w[l], alog, dtb)

        x = pl.pallas_call(
            _gdn_kernel,
            grid=grid,
            in_specs=[_tile_spec(D, ts)] + [_tile_spec(D, ts // 2)] * 6 + [
                _tile_spec(LANES, ts), _const_spec((1, HEAD_DIM)),
                _const_spec((D // 2, D)), _const_spec((D // 2, D))],
            out_specs=_tile_spec(D, ts),
            out_shape=tok,
            scratch_shapes=[pltpu.VMEM((HEADS, HEAD_DIM, HEAD_DIM), F32),
                            pltpu.VMEM((ts, D), BF16)],
            compiler_params=_params(),
            name="gdn_mix",
        )(x, mixa, sgb, zs, q, k, v, gb, gdn_norm_g[l].reshape(1, HEAD_DIM),
          _pack(w_b_out, l), _pack(w_o, l))

        x = pl.pallas_call(
            functools.partial(_ffn_kernel, final_norm=(l == depth - 1)),
            grid=grid,
            in_specs=[_tile_spec(D, ts), _const_spec((1, D)), _const_spec((D // 2, 2 * d_ff)),
                      _const_spec((CONV_K, 2 * d_ff)), _const_spec((d_ff // 2, D)), _const_spec((1, D))],
            out_specs=_tile_spec(D, ts),
            out_shape=tok,
            scratch_shapes=[pltpu.VMEM((CARRY_ROWS, 2 * d_ff), F32)],
            compiler_params=_params(),
            name="conv_ffn",
        )(x, norm_ffn_g[l].reshape(1, D), _pack(w_up, l, block_cols=512), ffn_conv_w[l],
          _pack(w_down, l, block_rows=704), norm_final_g.reshape(1, D))
    return x
```

```python
import functools

import jax
import jax.numpy as jnp
from jax import lax
from jax.experimental import pallas as pl
from jax.experimental.pallas import tpu as pltpu

D_MODEL = 1024
HEADS = 8
HEAD_DIM = 128
GDN_CONV_K = 4
CONV_K = 3
EPS = 1e-6
LANES = 128
CARRY_ROWS = 8
SEQ_TILE = 256
CHUNK = 64
COL_CHUNK = 256
MID_FIELDS = ("q", "k", "v", "zs", "mixa", "sgb")
VMEM_LIMIT = 52 * 1024 * 1024

F32 = jnp.float32
BF16 = jnp.bfloat16


def _dot(a, b):
    return jnp.dot(a, b, preferred_element_type=F32)


def _dot_nt(a, b):
    return lax.dot_general(a, b, (((1,), (1,)), ((), ())), preferred_element_type=F32)


def _dot_tn(a, b):
    return lax.dot_general(a, b, (((0,), (0,)), ((), ())), preferred_element_type=F32)


def _pack_kernel(w_ref, o_ref):
    o_ref[...] = _as_pairs(w_ref[...])


def _pack(w, layer=None, block_rows=None, block_cols=None, n_col_blocks=None):
    k, n = w.shape[-2:]
    bk = block_rows or k
    bn = block_cols or n
    ncb = n_col_blocks or n // bn
    if layer is None:
        in_spec = pl.BlockSpec((bk, bn), lambda i, j: (i, j))
    else:
        in_spec = pl.BlockSpec((None, bk, bn), lambda i, j: (layer, i, j))
    return pl.pallas_call(
        _pack_kernel,
        grid=(k // bk, ncb),
        in_specs=[in_spec],
        out_specs=pl.BlockSpec((bk // 2, bn), lambda i, j: (i, j)),
        out_shape=jax.ShapeDtypeStruct((k // 2, ncb * bn), jnp.int32),
        compiler_params=pltpu.CompilerParams(dimension_semantics=("arbitrary", "arbitrary")),
        name="pack_weight",
    )(w)


def _pack_t_kernel(wt_ref, o_ref):
    o_ref[...] = _as_pairs(wt_ref[...].T)


def _pack_t(wt, layer=None, block=1024, n_blocks=None):
    n, k = wt.shape[-2:]
    nb = n_blocks or n // block
    if layer is None:
        in_spec = pl.BlockSpec((block, k), lambda j: (j, 0))
    else:
        in_spec = pl.BlockSpec((None, block, k), lambda j: (layer, j, 0))
    return pl.pallas_call(
        _pack_t_kernel,
        grid=(nb,),
        in_specs=[in_spec],
        out_specs=pl.BlockSpec((k // 2, block), lambda j: (0, j)),
        out_shape=jax.ShapeDtypeStruct((k // 2, nb * block), jnp.int32),
        compiler_params=pltpu.CompilerParams(dimension_semantics=("arbitrary",)),
        name="pack_weight_t",
    )(wt)


def _unpack(x_i32):
    return pltpu.bitcast(x_i32, BF16)


def _as_pairs(x):
    return pltpu.bitcast(x.astype(BF16), jnp.int32)


def _sigmoid(x):
    return 0.5 * jnp.tanh(0.5 * x) + 0.5


def _rmsnorm(x, g):
    return x * lax.rsqrt(jnp.mean(x * x, axis=-1, keepdims=True) + EPS) * g


def _causal_conv(x, carry_ref, w):
    taps = w.shape[0]
    rows = x.shape[0]
    carry = carry_ref[...]
    sub = lax.broadcasted_iota(jnp.int32, carry.shape, 0)
    acc = w[taps - 1:taps, :] * x
    for d in range(1, taps):
        rolled = pltpu.roll(x, d, 0)
        head = jnp.where(sub < d, pltpu.roll(carry, d, 0), rolled[:CARRY_ROWS])
        shifted = jnp.concatenate([head, rolled[CARRY_ROWS:]], axis=0)
        acc = acc + w[taps - 1 - d:taps - d, :] * shifted
    carry_ref[...] = x[rows - CARRY_ROWS:, :]
    return acc


def _mid_views(mid_ref):
    return [mid_ref.at[:, :, i * D_MODEL:(i + 1) * D_MODEL] for i in range(len(MID_FIELDS))]


def _merge(a, b):
    out, i, j = [], 0, 0
    while i < len(a) or j < len(b):
        if j >= len(b) or (i < len(a) and i * len(b) <= j * len(a)):
            out.append(a[i])
            i += 1
        else:
            out.append(b[j])
            j += 1
    return out


def _split3(x):
    hi = x.astype(BF16)
    r = x - hi.astype(F32)
    mid = r.astype(BF16)
    lo = (r - mid.astype(F32)).astype(BF16)
    return hi, mid, lo


def _in_proj_kernel(x_ref, ng_ref, wmain_ref, wab_ref, wgate_ref, waout_ref,
                    cwa_ref, cwg_ref, alog_ref, dtb_ref,
                    mid_ref, gb_ref,
                    cbuf, qbuf, yabuf, sgabuf):
    q_ref, k_ref, v_ref, zs_ref, mixa_ref, sgb_ref = _mid_views(mid_ref)

    @pl.when(pl.program_id(1) == 0)
    def _():
        cbuf[...] = jnp.zeros(cbuf.shape, F32)
        qbuf[...] = jnp.zeros(qbuf.shape, F32)

    h = _rmsnorm(x_ref[0], ng_ref[...]).astype(BF16)

    def proj(block, cols):
        start = block * D_MODEL + cols.start
        return _dot(h, _unpack(wmain_ref[:, start:start + COL_CHUNK]))

    def conv_silu(group, cols):
        gcols = slice(group * D_MODEL + cols.start, group * D_MODEL + cols.stop)
        c = _causal_conv(proj(3 + group, cols), qbuf.at[:, gcols], cwg_ref[:, gcols])
        return c * _sigmoid(c)

    def l2norm_heads(c, scale, out_ref, cols):
        for lo in range(0, COL_CHUNK, HEAD_DIM):
            ch = c[:, lo:lo + HEAD_DIM]
            inv = lax.rsqrt(jnp.sum(ch * ch, axis=-1, keepdims=True) + EPS)
            out_ref[0, :, cols.start + lo:cols.start + lo + HEAD_DIM] = _as_pairs(
                ch * (inv * scale if scale != 1.0 else inv))

    for start in range(0, D_MODEL, COL_CHUNK):
        cols = slice(start, start + COL_CHUNK)
        l2norm_heads(conv_silu(0, cols), HEAD_DIM ** -0.5, q_ref, cols)
        conv = _causal_conv(proj(1, cols) * proj(2, cols), cbuf.at[:, cols], cwa_ref[:, cols])
        l2norm_heads(conv_silu(1, cols), 1.0, k_ref, cols)
        yabuf[:, cols] = (proj(0, cols) * conv).astype(BF16)
        sgabuf[:, cols] = _sigmoid(_dot(h, _unpack(wgate_ref[:, cols])))
        v_ref[0, :, cols] = _as_pairs(conv_silu(2, cols))
        sgb_ref[0, :, cols] = _as_pairs(_sigmoid(
            _dot(h, _unpack(wgate_ref[:, D_MODEL + start:D_MODEL + start + COL_CHUNK]))))
        pz = proj(6, cols)
        zs_ref[0, :, cols] = _as_pairs(pz * _sigmoid(pz))

    pab = _dot(h, _unpack(wab_ref[...]))
    sp = pab + dtb_ref[...]
    softplus = jnp.maximum(sp, 0.0) + jnp.log1p(jnp.exp(-jnp.abs(sp)))
    lane = lax.broadcasted_iota(jnp.int32, pab.shape, 1)
    gb_ref[0] = jnp.where(lane < HEADS, -jnp.exp(alog_ref[...]) * softplus, _sigmoid(pab))

    mixa_ref[0] = _as_pairs(sgabuf[...] * _dot(yabuf[...], _unpack(waout_ref[...])))


def _gdn_kernel(x_ref, mid_ref, gb_ref, gng_ref, wb_ref, wo_ref,
                x1_ref, state, ogbuf):
    ts = x_ref.shape[1]
    q_ref, k_ref, v_ref, zs_ref, mixa_ref, sgb_ref = _mid_views(mid_ref)

    @pl.when(pl.program_id(1) == 0)
    def _():
        state[...] = jnp.zeros(state.shape, F32)

    row = lax.broadcasted_iota(jnp.int32, (ts, ts), 0)
    col = lax.broadcasted_iota(jnp.int32, (ts, ts), 1)
    same = (row // CHUNK) == (col // CHUNK)
    tril = same & (col <= row)
    strict = same & (col < row)

    gbv = gb_ref[0]
    tril_b = tril.astype(BF16)
    same_b = same.astype(BF16)
    G = None
    GL = None
    for part in _split3(gbv):
        G = _dot(tril_b, part) if G is None else G + _dot(tril_b, part)
        GL = _dot(same_b, part) if GL is None else GL + _dot(same_b, part)
    GT = G.T
    eG = jnp.exp(G)
    eGLG = jnp.exp(GL - G)
    eGL = jnp.exp(GL)
    gng = gng_ref[...]

    lanes = [slice(hd * HEAD_DIM, (hd + 1) * HEAD_DIM) for hd in range(HEADS)]
    n_chunks = ts // CHUNK
    lane_chunk = lax.broadcasted_iota(jnp.int32, (CHUNK, ts), 1) // CHUNK
    eye_sbs = (lax.broadcasted_iota(jnp.int32, (CHUNK, ts), 0)
               == lax.broadcasted_iota(jnp.int32, (CHUNK, ts), 1) % CHUNK).astype(F32)

    def fold(m):
        acc = m[:CHUNK]
        for c in range(1, n_chunks):
            acc = acc + m[c * CHUNK:(c + 1) * CHUNK]
        return acc

    def unfold(m):
        return jnp.concatenate(
            [jnp.where(lane_chunk == c, m, jnp.zeros_like(m)) for c in range(n_chunks)], axis=0)

    st = [dict() for _ in range(HEADS)]

    def setup(group):
        def prep(hd):
            d = st[hd]
            q = _unpack(q_ref[0, :, lanes[hd]])
            k = _unpack(k_ref[0, :, lanes[hd]])
            v = _unpack(v_ref[0, :, lanes[hd]])
            beta = gbv[:, HEADS + hd:HEADS + hd + 1]
            eg = eG[:, hd:hd + 1]
            kf = k.astype(F32)
            kb = kf * beta
            d["q"], d["k"], d["kb"] = q, k, kb.astype(BF16)
            d["qd"] = (q.astype(F32) * eg).astype(BF16)
            d["kd"] = (kf * eGLG[:, hd:hd + 1]).astype(BF16)
            d["rhs"] = jnp.concatenate([v.astype(F32) * beta, kb * eg], axis=1).astype(BF16)
            d["decay"] = jnp.where(tril, jnp.exp(G[:, hd:hd + 1] - GT[hd:hd + 1, :]), 0.0)

        def products(hd):
            d = st[hd]
            d["L"] = jnp.where(strict, _dot_nt(d["kb"], d["k"]) * d["decay"], 0.0)
            d["A"] = (_dot_nt(d["q"], d["k"]) * d["decay"]).astype(BF16)

        return [functools.partial(f, hd) for hd in group for f in (prep, products)]

    def inverse(group):
        def first(hd):
            d = st[hd]
            l_sbs = fold(d["L"])
            d["T"] = eye_sbs - l_sbs
            d["P"] = _dot(l_sbs.astype(BF16), d["L"].astype(BF16))

        def both(hd):
            d = st[hd]
            p_b = d["P"].astype(BF16)
            tp = _dot(jnp.concatenate([d["T"].astype(BF16), p_b], axis=0), unfold(p_b))
            d["T"] = d["T"] + tp[:CHUNK]
            d["P"] = tp[CHUNK:]

        def last(hd):
            d = st[hd]
            d["T"] = d["T"] + _dot(d["T"].astype(BF16), unfold(d["P"].astype(BF16)))

        def solve(hd):
            d = st[hd]
            uw = _dot(unfold(d["T"].astype(BF16)), d["rhs"])
            d["u"] = uw[:, :HEAD_DIM]
            d["w"] = uw[:, HEAD_DIM:].astype(BF16)

        steps = [functools.partial(first, hd) for hd in group]
        n = 4
        while n < CHUNK:
            steps += [functools.partial(both, hd) for hd in group]
            n *= 2
        return steps + [functools.partial(f, hd) for f in (last, solve) for hd in group]

    def recurrence(group):
        def load(hd):
            st[hd]["S"] = state[hd]

        def apply_state(c, hd):
            d = st[hd]
            rows = slice(c * CHUNK, (c + 1) * CHUNK)
            d["wq"] = _dot(jnp.concatenate([d["w"][rows], d["qd"][rows]], axis=0), d["S"].astype(BF16))

        def update_state(c, hd):
            d = st[hd]
            rows = slice(c * CHUNK, (c + 1) * CHUNK)
            d["vn"] = (d["u"][rows] - d["wq"][:CHUNK]).astype(BF16)
            d["S"] = d["S"] * eGL[c * CHUNK:c * CHUNK + 1, hd:hd + 1] + _dot_tn(d["kd"][rows], d["vn"])

        def output(c, hd):
            d = st[hd]
            rows = slice(c * CHUNK, (c + 1) * CHUNK)
            pair_rows = slice(c * CHUNK // 2, (c + 1) * CHUNK // 2)
            o = d["wq"][CHUNK:] + _dot(d["A"][rows, rows], d["vn"])
            o = o * lax.rsqrt(jnp.mean(o * o, axis=-1, keepdims=True) + EPS) * gng
            zs = _unpack(zs_ref[0, pair_rows, lanes[hd]]).astype(F32)
            ogbuf[rows, lanes[hd]] = (o * zs).astype(BF16)

        def store(hd):
            state[hd] = st[hd]["S"]

        steps = [functools.partial(load, hd) for hd in group]
        for c in range(n_chunks):
            for f in (apply_state, update_state, output):
                steps += [functools.partial(f, c, hd) for hd in group]
        return steps + [functools.partial(store, hd) for hd in group]

    y_b = []

    def project(group):
        cols = slice(group[0] * HEAD_DIM, (group[-1] + 1) * HEAD_DIM)
        pair_cols = slice(cols.start // 2, cols.stop // 2)

        def piece(start):
            y_b.append(_dot(ogbuf[:, cols], _unpack(wb_ref[pair_cols, start:start + 2 * HEAD_DIM])))

        return [functools.partial(piece, start) for start in range(0, D_MODEL, 2 * HEAD_DIM)]

    def run(steps):
        for step in steps:
            step()

    heads = list(range(HEADS))
    run(setup(heads))
    run(inverse(heads))
    run(recurrence(heads))
    run(project(heads))
    y_b = jnp.concatenate(y_b, axis=1)
    mix = _unpack(mixa_ref[0]).astype(F32) + _unpack(sgb_ref[0]).astype(F32) * y_b
    x1_ref[0] = x_ref[0] + _dot(mix.astype(BF16), _unpack(wo_ref[...]))


def _ffn_kernel(x1_ref, ng_ref, wup_ref, cw_ref, wdown_ref, nf_ref, out_ref, ubuf, *, final_norm):
    d_ff = 2 * wdown_ref.shape[0]

    @pl.when(pl.program_id(1) == 0)
    def _():
        ubuf[...] = jnp.zeros(ubuf.shape, F32)

    x1 = x1_ref[0]
    h = _rmsnorm(x1, ng_ref[...]).astype(BF16)
    up = _causal_conv(_dot(h, _unpack(wup_ref[...])), ubuf, cw_ref[...])
    gate = up[:, :d_ff]
    act = (gate * _sigmoid(gate) * up[:, d_ff:]).astype(BF16)
    x2 = x1 + _dot(act, _unpack(wdown_ref[...]))
    out_ref[0] = _rmsnorm(x2, nf_ref[...]) if final_norm else x2


def _const_spec(shape):
    return pl.BlockSpec(shape, lambda b, s: (0,) * len(shape), pipeline_mode=pl.Buffered(1))


def _tile_spec(width, rows):
    return pl.BlockSpec((1, rows, width), lambda b, s: (b, s, 0))


def _params():
    return pltpu.CompilerParams(dimension_semantics=("arbitrary", "arbitrary"),
                                vmem_limit_bytes=VMEM_LIMIT)


def kernel(x, norm_mix_g, w_in, conv_a_w, gdn_conv_w, gdn_A_log, gdn_dt_bias, gdn_norm_g,
           w_a_out, w_b_out, w_o, norm_ffn_g, w_up, ffn_conv_w, w_down, norm_final_g):
    B, S, D = x.shape
    depth = w_in.shape[0]
    ts = SEQ_TILE
    grid = (B, S // ts)
    d_ff = w_down.shape[1]
    tok = jax.ShapeDtypeStruct((B, S, D), F32)
    mid_shape = jax.ShapeDtypeStruct((B, S // 2, len(MID_FIELDS) * D), jnp.int32)

    for l in range(depth):
        w_in_t = jnp.swapaxes(w_in, 1, 2)
        w_main = _pack_t(w_in_t, l, n_blocks=7)
        w_ab = _pack(jnp.pad(w_in_t[l, 7 * D:7 * D + 2 * HEADS, :].T, ((0, 0), (0, LANES - 2 * HEADS))))
        w_gate = _pack_t(w_in_t[l, 7 * D + 2 * HEADS:, :])
        alog = jnp.pad(gdn_A_log[l], (0, LANES - HEADS)).reshape(1, LANES)
        dtb = jnp.pad(gdn_dt_bias[l], (0, LANES - HEADS)).reshape(1, LANES)

        mid, gb = pl.pallas_call(
            _in_proj_kernel,
            grid=grid,
            in_specs=[_tile_spec(D, ts), _const_spec((1, D)),
                      _const_spec((D // 2, 7 * D)),
                      _const_spec((D // 2, LANES)), _const_spec((D // 2, 2 * D)), _const_spec((D // 2, D)),
                      _const_spec((CONV_K, D)), _const_spec((GDN_CONV_K, 3 * D)),
                      _const_spec((1, LANES)), _const_spec((1, LANES))],
            out_specs=[_tile_spec(len(MID_FIELDS) * D, ts // 2), _tile_spec(LANES, ts)],
            out_shape=[mid_shape, jax.ShapeDtypeStruct((B, S, LANES), F32)],
            scratch_shapes=[pltpu.VMEM((CARRY_ROWS, D), F32),
                            pltpu.VMEM((CARRY_ROWS, 3 * D), F32),
                            pltpu.VMEM((ts, D), BF16), pltpu.VMEM((ts, D), F32)],
            compiler_params=_params(),
            name="in_proj",
        )(x, norm_mix_g[l].reshape(1, D), w_main, w_ab, w_gate, _pack(w_a_out, l),
          conv_a_w[l], gdn_conv_w[l], alog, dtb)

        x = pl.pallas_call(
            _gdn_kernel,
            grid=grid,
            in_specs=[_tile_spec(D, ts), _tile_spec(len(MID_FIELDS) * D, ts // 2),
                      _tile_spec(LANES, ts), _const_spec((1, HEAD_DIM)),
                _const_spec((D // 2, D)), _const_spec((D // 2, D))],
            out_specs=_tile_spec(D, ts),
            out_shape=tok,
            scratch_shapes=[pltpu.VMEM((HEADS, HEAD_DIM, HEAD_DIM), F32),
                            pltpu.VMEM((ts, D), BF16)],
            compiler_params=_params(),
            name="gdn_mix",
        )(x, mid, gb, gdn_norm_g[l].reshape(1, HEAD_DIM),
          _pack(w_b_out, l), _pack(w_o, l))

        x = pl.pallas_call(
            functools.partial(_ffn_kernel, final_norm=(l == depth - 1)),
            grid=grid,
            in_specs=[_tile_spec(D, ts), _const_spec((1, D)), _const_spec((D // 2, 2 * d_ff)),
                      _const_spec((CONV_K, 2 * d_ff)), _const_spec((d_ff // 2, D)), _const_spec((1, D))],
            out_specs=_tile_spec(D, ts),
            out_shape=tok,
            scratch_shapes=[pltpu.VMEM((CARRY_ROWS, 2 * d_ff), F32)],
            compiler_params=_params(),
            name="conv_ffn",
        )(x, norm_ffn_g[l].reshape(1, D), _pack(w_up, l, block_cols=512), ffn_conv_w[l],
          _pack(w_down, l, block_rows=704), norm_final_g.reshape(1, D))
    return x
```

```python
import functools

import jax
import jax.numpy as jnp
from jax import lax
from jax.experimental import pallas as pl
from jax.experimental.pallas import tpu as pltpu

D_MODEL = 1024
HEADS = 8
HEAD_DIM = 128
GDN_CONV_K = 4
CONV_K = 3
EPS = 1e-6
LANES = 128
CARRY_ROWS = 8
SEQ_TILE = 256
PROJ_TILE = 512
CHUNK = 64
COL_CHUNK = 256
DELTA_FIELDS = ("q", "k", "v", "zs")
MERGE_FIELDS = ("mixa", "sgb")
VMEM_LIMIT = 52 * 1024 * 1024

F32 = jnp.float32
BF16 = jnp.bfloat16


def _dot(a, b):
    return jnp.dot(a, b, preferred_element_type=F32)


def _dot_nt(a, b):
    return lax.dot_general(a, b, (((1,), (1,)), ((), ())), preferred_element_type=F32)


def _dot_tn(a, b):
    return lax.dot_general(a, b, (((0,), (0,)), ((), ())), preferred_element_type=F32)


def _pack_kernel(w_ref, o_ref):
    o_ref[...] = _as_pairs(w_ref[...])


def _pack(w, layer=None, block_rows=None, block_cols=None, n_col_blocks=None):
    k, n = w.shape[-2:]
    bk = block_rows or k
    bn = block_cols or n
    ncb = n_col_blocks or n // bn
    if layer is None:
        in_spec = pl.BlockSpec((bk, bn), lambda i, j: (i, j))
    else:
        in_spec = pl.BlockSpec((None, bk, bn), lambda i, j: (layer, i, j))
    return pl.pallas_call(
        _pack_kernel,
        grid=(k // bk, ncb),
        in_specs=[in_spec],
        out_specs=pl.BlockSpec((bk // 2, bn), lambda i, j: (i, j)),
        out_shape=jax.ShapeDtypeStruct((k // 2, ncb * bn), jnp.int32),
        compiler_params=pltpu.CompilerParams(dimension_semantics=("arbitrary", "arbitrary")),
        name="pack_weight",
    )(w)


def _pack_t_kernel(wt_ref, o_ref):
    o_ref[...] = _as_pairs(wt_ref[...].T)


def _pack_t(wt, layer=None, block=1024, n_blocks=None):
    n, k = wt.shape[-2:]
    nb = n_blocks or n // block
    if layer is None:
        in_spec = pl.BlockSpec((block, k), lambda j: (j, 0))
    else:
        in_spec = pl.BlockSpec((None, block, k), lambda j: (layer, j, 0))
    return pl.pallas_call(
        _pack_t_kernel,
        grid=(nb,),
        in_specs=[in_spec],
        out_specs=pl.BlockSpec((k // 2, block), lambda j: (0, j)),
        out_shape=jax.ShapeDtypeStruct((k // 2, nb * block), jnp.int32),
        compiler_params=pltpu.CompilerParams(dimension_semantics=("arbitrary",)),
        name="pack_weight_t",
    )(wt)


def _unpack(x_i32):
    return pltpu.bitcast(x_i32, BF16)


def _as_pairs(x):
    return pltpu.bitcast(x.astype(BF16), jnp.int32)


def _sigmoid(x):
    return 0.5 * jnp.tanh(0.5 * x) + 0.5


def _rmsnorm(x, g):
    return x * lax.rsqrt(jnp.mean(x * x, axis=-1, keepdims=True) + EPS) * g


def _causal_conv(x, carry_ref, w):
    taps = w.shape[0]
    rows = x.shape[0]
    carry = carry_ref[...]
    sub = lax.broadcasted_iota(jnp.int32, carry.shape, 0)
    acc = w[taps - 1:taps, :] * x
    for d in range(1, taps):
        rolled = pltpu.roll(x, d, 0)
        head = jnp.where(sub < d, pltpu.roll(carry, d, 0), rolled[:CARRY_ROWS])
        shifted = jnp.concatenate([head, rolled[CARRY_ROWS:]], axis=0)
        acc = acc + w[taps - 1 - d:taps - d, :] * shifted
    carry_ref[...] = x[rows - CARRY_ROWS:, :]
    return acc


def _field_views(ref, fields):
    return [ref.at[:, :, i * D_MODEL:(i + 1) * D_MODEL] for i in range(len(fields))]


def _merge(a, b):
    out, i, j = [], 0, 0
    while i < len(a) or j < len(b):
        if j >= len(b) or (i < len(a) and i * len(b) <= j * len(a)):
            out.append(a[i])
            i += 1
        else:
            out.append(b[j])
            j += 1
    return out


def _split3(x):
    hi = x.astype(BF16)
    r = x - hi.astype(F32)
    mid = r.astype(BF16)
    lo = (r - mid.astype(F32)).astype(BF16)
    return hi, mid, lo


def _in_proj_kernel(x_ref, ng_ref, wmain_ref, wab_ref, wgate_ref, waout_ref,
                    cwa_ref, cwg_ref, alog_ref, dtb_ref,
                    delta_ref, merge_ref, gb_ref,
                    cbuf, qbuf, yabuf, sgabuf):
    q_ref, k_ref, v_ref, zs_ref = _field_views(delta_ref, DELTA_FIELDS)
    mixa_ref, sgb_ref = _field_views(merge_ref, MERGE_FIELDS)

    @pl.when(pl.program_id(1) == 0)
    def _():
        cbuf[...] = jnp.zeros(cbuf.shape, F32)
        qbuf[...] = jnp.zeros(qbuf.shape, F32)

    h = _rmsnorm(x_ref[0], ng_ref[...]).astype(BF16)

    def proj(block, cols):
        start = block * D_MODEL + cols.start
        return _dot(h, _unpack(wmain_ref[:, start:start + COL_CHUNK]))

    def conv_silu(group, cols):
        gcols = slice(group * D_MODEL + cols.start, group * D_MODEL + cols.stop)
        c = _causal_conv(proj(3 + group, cols), qbuf.at[:, gcols], cwg_ref[:, gcols])
        return c * _sigmoid(c)

    def l2norm_heads(c, scale, out_ref, cols):
        for lo in range(0, COL_CHUNK, HEAD_DIM):
            ch = c[:, lo:lo + HEAD_DIM]
            inv = lax.rsqrt(jnp.sum(ch * ch, axis=-1, keepdims=True) + EPS)
            out_ref[0, :, cols.start + lo:cols.start + lo + HEAD_DIM] = _as_pairs(
                ch * (inv * scale if scale != 1.0 else inv))

    for start in range(0, D_MODEL, COL_CHUNK):
        cols = slice(start, start + COL_CHUNK)
        l2norm_heads(conv_silu(0, cols), HEAD_DIM ** -0.5, q_ref, cols)
        conv = _causal_conv(proj(1, cols) * proj(2, cols), cbuf.at[:, cols], cwa_ref[:, cols])
        l2norm_heads(conv_silu(1, cols), 1.0, k_ref, cols)
        yabuf[:, cols] = (proj(0, cols) * conv).astype(BF16)
        sgabuf[:, cols] = _sigmoid(_dot(h, _unpack(wgate_ref[:, cols])))
        v_ref[0, :, cols] = _as_pairs(conv_silu(2, cols))
        sgb_ref[0, :, cols] = _as_pairs(_sigmoid(
            _dot(h, _unpack(wgate_ref[:, D_MODEL + start:D_MODEL + start + COL_CHUNK]))))
        pz = proj(6, cols)
        zs_ref[0, :, cols] = _as_pairs(pz * _sigmoid(pz))

    pab = _dot(h, _unpack(wab_ref[...]))
    sp = pab + dtb_ref[...]
    softplus = jnp.maximum(sp, 0.0) + jnp.log1p(jnp.exp(-jnp.abs(sp)))
    lane = lax.broadcasted_iota(jnp.int32, pab.shape, 1)
    gb_ref[0] = jnp.where(lane < HEADS, -jnp.exp(alog_ref[...]) * softplus, _sigmoid(pab))

    mixa_ref[0] = _as_pairs(sgabuf[...] * _dot(yabuf[...], _unpack(waout_ref[...])))


def _gdn_kernel(x_ref, delta_ref, merge_ref, gb_ref, gng_ref, wb_ref, wo_ref,
                x1_ref, state, ogbuf, mixbuf):
    ts = x_ref.shape[1]
    q_ref, k_ref, v_ref, zs_ref = _field_views(delta_ref, DELTA_FIELDS)
    mixa_ref, sgb_ref = _field_views(merge_ref, MERGE_FIELDS)

    @pl.when(pl.program_id(1) == 0)
    def _():
        state[...] = jnp.zeros(state.shape, F32)
        ogbuf[...] = jnp.zeros(ogbuf.shape, BF16)

    slot = pl.program_id(1) % 2
    og_new = ogbuf.at[slot]
    og_old = ogbuf.at[1 - slot]

    row = lax.broadcasted_iota(jnp.int32, (ts, ts), 0)
    col = lax.broadcasted_iota(jnp.int32, (ts, ts), 1)
    same = (row // CHUNK) == (col // CHUNK)
    tril = same & (col <= row)
    strict = same & (col < row)

    gbv = gb_ref[0]
    tril_b = tril.astype(BF16)
    same_b = same.astype(BF16)
    G = None
    GL = None
    for part in _split3(gbv):
        G = _dot(tril_b, part) if G is None else G + _dot(tril_b, part)
        GL = _dot(same_b, part) if GL is None else GL + _dot(same_b, part)
    GT = G.T
    eG = jnp.exp(G)
    eGLG = jnp.exp(GL - G)
    eGL = jnp.exp(GL)
    gng = gng_ref[...]

    lanes = [slice(hd * HEAD_DIM, (hd + 1) * HEAD_DIM) for hd in range(HEADS)]
    n_chunks = ts // CHUNK
    lane_chunk = lax.broadcasted_iota(jnp.int32, (CHUNK, ts), 1) // CHUNK
    eye_sbs = (lax.broadcasted_iota(jnp.int32, (CHUNK, ts), 0)
               == lax.broadcasted_iota(jnp.int32, (CHUNK, ts), 1) % CHUNK).astype(F32)

    def fold(m):
        acc = m[:CHUNK]
        for c in range(1, n_chunks):
            acc = acc + m[c * CHUNK:(c + 1) * CHUNK]
        return acc

    def unfold(m):
        return jnp.concatenate(
            [jnp.where(lane_chunk == c, m, jnp.zeros_like(m)) for c in range(n_chunks)], axis=0)

    st = [dict() for _ in range(HEADS)]

    def setup(group):
        def prep(hd):
            d = st[hd]
            q = _unpack(q_ref[0, :, lanes[hd]])
            k = _unpack(k_ref[0, :, lanes[hd]])
            v = _unpack(v_ref[0, :, lanes[hd]])
            beta = gbv[:, HEADS + hd:HEADS + hd + 1]
            eg = eG[:, hd:hd + 1]
            kf = k.astype(F32)
            kb = kf * beta
            d["q"], d["k"], d["kb"] = q, k, kb.astype(BF16)
            d["qd"] = (q.astype(F32) * eg).astype(BF16)
            d["kd"] = (kf * eGLG[:, hd:hd + 1]).astype(BF16)
            d["rhs"] = jnp.concatenate([v.astype(F32) * beta, kb * eg], axis=1).astype(BF16)
            d["decay"] = jnp.where(tril, jnp.exp(G[:, hd:hd + 1] - GT[hd:hd + 1, :]), 0.0)

        def products(hd):
            d = st[hd]
            d["L"] = jnp.where(strict, _dot_nt(d["kb"], d["k"]) * d["decay"], 0.0)
            d["A"] = (_dot_nt(d["q"], d["k"]) * d["decay"]).astype(BF16)

        return [functools.partial(f, hd) for hd in group for f in (prep, products)]

    def inverse(group):
        def first(hd):
            d = st[hd]
            l_sbs = fold(d["L"])
            d["T"] = eye_sbs - l_sbs
            d["P"] = _dot(l_sbs.astype(BF16), d["L"].astype(BF16))

        def both(hd):
            d = st[hd]
            p_b = d["P"].astype(BF16)
            tp = _dot(jnp.concatenate([d["T"].astype(BF16), p_b], axis=0), unfold(p_b))
            d["T"] = d["T"] + tp[:CHUNK]
            d["P"] = tp[CHUNK:]

        def last(hd):
            d = st[hd]
            d["T"] = d["T"] + _dot(d["T"].astype(BF16), unfold(d["P"].astype(BF16)))

        def solve(hd):
            d = st[hd]
            uw = _dot(unfold(d["T"].astype(BF16)), d["rhs"])
            d["u"] = uw[:, :HEAD_DIM]
            d["w"] = uw[:, HEAD_DIM:].astype(BF16)

        steps = [functools.partial(first, hd) for hd in group]
        n = 4
        while n < CHUNK:
            steps += [functools.partial(both, hd) for hd in group]
            n *= 2
        return steps + [functools.partial(f, hd) for f in (last, solve) for hd in group]

    def recurrence(group):
        def load(hd):
            st[hd]["S"] = state[hd]

        def apply_state(c, hd):
            d = st[hd]
            rows = slice(c * CHUNK, (c + 1) * CHUNK)
            d["wq"] = _dot(jnp.concatenate([d["w"][rows], d["qd"][rows]], axis=0), d["S"].astype(BF16))

        def update_state(c, hd):
            d = st[hd]
            rows = slice(c * CHUNK, (c + 1) * CHUNK)
            d["vn"] = (d["u"][rows] - d["wq"][:CHUNK]).astype(BF16)
            d["S"] = d["S"] * eGL[c * CHUNK:c * CHUNK + 1, hd:hd + 1] + _dot_tn(d["kd"][rows], d["vn"])

        def output(c, hd):
            d = st[hd]
            rows = slice(c * CHUNK, (c + 1) * CHUNK)
            pair_rows = slice(c * CHUNK // 2, (c + 1) * CHUNK // 2)
            o = d["wq"][CHUNK:] + _dot(d["A"][rows, rows], d["vn"])
            o = o * lax.rsqrt(jnp.mean(o * o, axis=-1, keepdims=True) + EPS) * gng
            zs = _unpack(zs_ref[0, pair_rows, lanes[hd]]).astype(F32)
            og_new[rows, lanes[hd]] = (o * zs).astype(BF16)

        def store(hd):
            state[hd] = st[hd]["S"]

        steps = [functools.partial(load, hd) for hd in group]
        for c in range(n_chunks):
            for f in (apply_state, update_state, output):
                steps += [functools.partial(f, c, hd) for hd in group]
        return steps + [functools.partial(store, hd) for hd in group]

    def merge_epilogue():
        def mix(cols):
            y_b = _dot(og_old[...], _unpack(wb_ref[:, cols]))
            m = _unpack(mixa_ref[0, :, cols]).astype(F32) + _unpack(sgb_ref[0, :, cols]).astype(F32) * y_b
            mixbuf[:, cols] = m.astype(BF16)

        def out(cols):
            x1_ref[0, :, cols] = x_ref[0, :, cols] + _dot(mixbuf[...], _unpack(wo_ref[:, cols]))

        pieces = [slice(start, start + COL_CHUNK) for start in range(0, D_MODEL, COL_CHUNK)]
        return [functools.partial(f, cols) for f in (mix, out) for cols in pieces]

    def run(steps):
        for step in steps:
            step()

    heads = list(range(HEADS))
    run(setup(heads))
    run(_merge(inverse(heads) + recurrence(heads), merge_epilogue()))


def _ffn_kernel(x1_ref, ng_ref, wup_ref, cw_ref, wdown_ref, nf_ref, out_ref, ubuf, *, final_norm):
    d_ff = 2 * wdown_ref.shape[0]

    @pl.when(pl.program_id(1) == 0)
    def _():
        ubuf[...] = jnp.zeros(ubuf.shape, F32)

    x1 = x1_ref[0]
    h = _rmsnorm(x1, ng_ref[...]).astype(BF16)
    up = _causal_conv(_dot(h, _unpack(wup_ref[...])), ubuf, cw_ref[...])
    gate = up[:, :d_ff]
    act = (gate * _sigmoid(gate) * up[:, d_ff:]).astype(BF16)
    x2 = x1 + _dot(act, _unpack(wdown_ref[...]))
    out_ref[0] = _rmsnorm(x2, nf_ref[...]) if final_norm else x2


def _const_spec(shape):
    return pl.BlockSpec(shape, lambda b, s: (0,) * len(shape), pipeline_mode=pl.Buffered(1))


def _tile_spec(width, rows):
    return pl.BlockSpec((1, rows, width), lambda b, s: (b, s, 0))


def _params():
    return pltpu.CompilerParams(dimension_semantics=("arbitrary", "arbitrary"),
                                vmem_limit_bytes=VMEM_LIMIT)


def kernel(x, norm_mix_g, w_in, conv_a_w, gdn_conv_w, gdn_A_log, gdn_dt_bias, gdn_norm_g,
           w_a_out, w_b_out, w_o, norm_ffn_g, w_up, ffn_conv_w, w_down, norm_final_g):
    B, S, D = x.shape
    depth = w_in.shape[0]
    ts = SEQ_TILE
    tp = PROJ_TILE
    n_tiles = S // ts
    grid = (B, n_tiles)
    d_ff = w_down.shape[1]
    tok = jax.ShapeDtypeStruct((B, S, D), F32)
    n_delta, n_merge = len(DELTA_FIELDS) * D, len(MERGE_FIELDS) * D
    pairs = lambda width: jax.ShapeDtypeStruct((B, S // 2, width), jnp.int32)
    cur_tile = lambda b, s: (b, jnp.minimum(s, n_tiles - 1), 0)
    prev_tile = lambda b, s: (b, jnp.maximum(s - 1, 0), 0)

    for l in range(depth):
        w_in_t = jnp.swapaxes(w_in, 1, 2)
        w_main = _pack_t(w_in_t, l, n_blocks=7)
        w_ab = _pack(jnp.pad(w_in_t[l, 7 * D:7 * D + 2 * HEADS, :].T, ((0, 0), (0, LANES - 2 * HEADS))))
        w_gate = _pack_t(w_in_t[l, 7 * D + 2 * HEADS:, :])
        alog = jnp.pad(gdn_A_log[l], (0, LANES - HEADS)).reshape(1, LANES)
        dtb = jnp.pad(gdn_dt_bias[l], (0, LANES - HEADS)).reshape(1, LANES)

        delta_in, merge_in, gb = pl.pallas_call(
            _in_proj_kernel,
            grid=(B, S // tp),
            in_specs=[_tile_spec(D, tp), _const_spec((1, D)),
                      _const_spec((D // 2, 7 * D)),
                      _const_spec((D // 2, LANES)), _const_spec((D // 2, 2 * D)), _const_spec((D // 2, D)),
                      _const_spec((CONV_K, D)), _const_spec((GDN_CONV_K, 3 * D)),
                      _const_spec((1, LANES)), _const_spec((1, LANES))],
            out_specs=[_tile_spec(n_delta, tp // 2), _tile_spec(n_merge, tp // 2), _tile_spec(LANES, tp)],
            out_shape=[pairs(n_delta), pairs(n_merge), jax.ShapeDtypeStruct((B, S, LANES), F32)],
            scratch_shapes=[pltpu.VMEM((CARRY_ROWS, D), F32),
                            pltpu.VMEM((CARRY_ROWS, 3 * D), F32),
                            pltpu.VMEM((tp, D), BF16), pltpu.VMEM((tp, D), F32)],
            compiler_params=_params(),
            name="in_proj",
        )(x, norm_mix_g[l].reshape(1, D), w_main, w_ab, w_gate, _pack(w_a_out, l),
          conv_a_w[l], gdn_conv_w[l], alog, dtb)

        x = pl.pallas_call(
            _gdn_kernel,
            grid=(B, n_tiles + 1),
            in_specs=[pl.BlockSpec((1, ts, D), prev_tile),
                      pl.BlockSpec((1, ts // 2, n_delta), cur_tile),
                      pl.BlockSpec((1, ts // 2, n_merge), prev_tile),
                      pl.BlockSpec((1, ts, LANES), cur_tile), _const_spec((1, HEAD_DIM)),
                _const_spec((D // 2, D)), _const_spec((D // 2, D))],
            out_specs=pl.BlockSpec((1, ts, D), prev_tile),
            out_shape=tok,
            scratch_shapes=[pltpu.VMEM((HEADS, HEAD_DIM, HEAD_DIM), F32),
                            pltpu.VMEM((2, ts, D), BF16), pltpu.VMEM((ts, D), BF16)],
            compiler_params=_params(),
            name="gdn_mix",
        )(x, delta_in, merge_in, gb, gdn_norm_g[l].reshape(1, HEAD_DIM),
          _pack(w_b_out, l), _pack(w_o, l))

        x = pl.pallas_call(
            functools.partial(_ffn_kernel, final_norm=(l == depth - 1)),
            grid=grid,
            in_specs=[_tile_spec(D, ts), _const_spec((1, D)), _const_spec((D // 2, 2 * d_ff)),
                      _const_spec((CONV_K, 2 * d_ff)), _const_spec((d_ff // 2, D)), _const_spec((1, D))],
            out_specs=_tile_spec(D, ts),
            out_shape=tok,
            scratch_shapes=[pltpu.VMEM((CARRY_ROWS, 2 * d_ff), F32)],
            compiler_params=_params(),
            name="conv_ffn",
        )(x, norm_ffn_g[l].reshape(1, D), _pack(w_up, l, block_cols=512), ffn_conv_w[l],
          _pack(w_down, l, block_rows=704), norm_final_g.reshape(1, D))
    return x
```

```python
import functools

import jax
import jax.numpy as jnp
from jax import lax
from jax.experimental import pallas as pl
from jax.experimental.pallas import tpu as pltpu

D_MODEL = 1024
HEADS = 8
HEAD_DIM = 128
GDN_CONV_K = 4
CONV_K = 3
EPS = 1e-6
LANES = 128
CARRY_ROWS = 8
SEQ_TILE = 256
CHUNK = 64
COL_CHUNK = 256
DELTA_FIELDS = ("q", "k", "v", "zs")
MERGE_FIELDS = ("mixa", "sgb")
VMEM_LIMIT = 52 * 1024 * 1024

F32 = jnp.float32
BF16 = jnp.bfloat16


def _dot(a, b):
    return jnp.dot(a, b, preferred_element_type=F32)


def _dot_nt(a, b):
    return lax.dot_general(a, b, (((1,), (1,)), ((), ())), preferred_element_type=F32)


def _dot_tn(a, b):
    return lax.dot_general(a, b, (((0,), (0,)), ((), ())), preferred_element_type=F32)


def _pack_kernel(w_ref, o_ref):
    o_ref[...] = _as_pairs(w_ref[...])


def _pack(w, layer=None, block_rows=None, block_cols=None, n_col_blocks=None):
    k, n = w.shape[-2:]
    bk = block_rows or k
    bn = block_cols or n
    ncb = n_col_blocks or n // bn
    if layer is None:
        in_spec = pl.BlockSpec((bk, bn), lambda i, j: (i, j))
    else:
        in_spec = pl.BlockSpec((None, bk, bn), lambda i, j: (layer, i, j))
    return pl.pallas_call(
        _pack_kernel,
        grid=(k // bk, ncb),
        in_specs=[in_spec],
        out_specs=pl.BlockSpec((bk // 2, bn), lambda i, j: (i, j)),
        out_shape=jax.ShapeDtypeStruct((k // 2, ncb * bn), jnp.int32),
        compiler_params=pltpu.CompilerParams(dimension_semantics=("arbitrary", "arbitrary")),
        name="pack_weight",
    )(w)


def _pack_t_kernel(wt_ref, o_ref):
    o_ref[...] = _as_pairs(wt_ref[...].T)


def _pack_t(wt, layer=None, block=1024, n_blocks=None):
    n, k = wt.shape[-2:]
    nb = n_blocks or n // block
    if layer is None:
        in_spec = pl.BlockSpec((block, k), lambda j: (j, 0))
    else:
        in_spec = pl.BlockSpec((None, block, k), lambda j: (layer, j, 0))
    return pl.pallas_call(
        _pack_t_kernel,
        grid=(nb,),
        in_specs=[in_spec],
        out_specs=pl.BlockSpec((k // 2, block), lambda j: (0, j)),
        out_shape=jax.ShapeDtypeStruct((k // 2, nb * block), jnp.int32),
        compiler_params=pltpu.CompilerParams(dimension_semantics=("arbitrary",)),
        name="pack_weight_t",
    )(wt)


def _unpack(x_i32):
    return pltpu.bitcast(x_i32, BF16)


def _as_pairs(x):
    return pltpu.bitcast(x.astype(BF16), jnp.int32)


def _sigmoid(x):
    return 0.5 * jnp.tanh(0.5 * x) + 0.5


def _rmsnorm(x, g):
    return x * lax.rsqrt(jnp.mean(x * x, axis=-1, keepdims=True) + EPS) * g


def _causal_conv(x, carry_ref, w):
    taps = w.shape[0]
    rows = x.shape[0]
    carry = carry_ref[...]
    sub = lax.broadcasted_iota(jnp.int32, carry.shape, 0)
    acc = w[taps - 1:taps, :] * x
    for d in range(1, taps):
        rolled = pltpu.roll(x, d, 0)
        head = jnp.where(sub < d, pltpu.roll(carry, d, 0), rolled[:CARRY_ROWS])
        shifted = jnp.concatenate([head, rolled[CARRY_ROWS:]], axis=0)
        acc = acc + w[taps - 1 - d:taps - d, :] * shifted
    carry_ref[...] = x[rows - CARRY_ROWS:, :]
    return acc


def _field_views(ref, fields):
    return [ref.at[:, :, i * D_MODEL:(i + 1) * D_MODEL] for i in range(len(fields))]


def _merge(a, b):
    out, i, j = [], 0, 0
    while i < len(a) or j < len(b):
        if j >= len(b) or (i < len(a) and i * len(b) <= j * len(a)):
            out.append(a[i])
            i += 1
        else:
            out.append(b[j])
            j += 1
    return out


def _split3(x):
    hi = x.astype(BF16)
    r = x - hi.astype(F32)
    mid = r.astype(BF16)
    lo = (r - mid.astype(F32)).astype(BF16)
    return hi, mid, lo


def _in_proj_kernel(x_ref, ng_ref, wmain_ref, wab_ref, wgate_ref, waout_ref,
                    cwa_ref, cwg_ref, alog_ref, dtb_ref,
                    delta_ref, merge_ref, gb_ref,
                    cbuf, qbuf, yabuf, sgabuf):
    q_ref, k_ref, v_ref, zs_ref = _field_views(delta_ref, DELTA_FIELDS)
    mixa_ref, sgb_ref = _field_views(merge_ref, MERGE_FIELDS)

    @pl.when(pl.program_id(1) == 0)
    def _():
        cbuf[...] = jnp.zeros(cbuf.shape, F32)
        qbuf[...] = jnp.zeros(qbuf.shape, F32)

    h = _rmsnorm(x_ref[0], ng_ref[...]).astype(BF16)

    def proj(block, cols):
        start = block * D_MODEL + cols.start
        return _dot(h, _unpack(wmain_ref[:, start:start + COL_CHUNK]))

    def conv_silu(group, cols):
        gcols = slice(group * D_MODEL + cols.start, group * D_MODEL + cols.stop)
        c = _causal_conv(proj(3 + group, cols), qbuf.at[:, gcols], cwg_ref[:, gcols])
        return c * _sigmoid(c)

    def l2norm_heads(c, scale, out_ref, cols):
        for lo in range(0, COL_CHUNK, HEAD_DIM):
            ch = c[:, lo:lo + HEAD_DIM]
            inv = lax.rsqrt(jnp.sum(ch * ch, axis=-1, keepdims=True) + EPS)
            out_ref[0, :, cols.start + lo:cols.start + lo + HEAD_DIM] = _as_pairs(
                ch * (inv * scale if scale != 1.0 else inv))

    for start in range(0, D_MODEL, COL_CHUNK):
        cols = slice(start, start + COL_CHUNK)
        l2norm_heads(conv_silu(0, cols), HEAD_DIM ** -0.5, q_ref, cols)
        conv = _causal_conv(proj(1, cols) * proj(2, cols), cbuf.at[:, cols], cwa_ref[:, cols])
        l2norm_heads(conv_silu(1, cols), 1.0, k_ref, cols)
        yabuf[:, cols] = (proj(0, cols) * conv).astype(BF16)
        sgabuf[:, cols] = _sigmoid(_dot(h, _unpack(wgate_ref[:, cols])))
        v_ref[0, :, cols] = _as_pairs(conv_silu(2, cols))
        sgb_ref[0, :, cols] = _as_pairs(_sigmoid(
            _dot(h, _unpack(wgate_ref[:, D_MODEL + start:D_MODEL + start + COL_CHUNK]))))
        pz = proj(6, cols)
        zs_ref[0, :, cols] = _as_pairs(pz * _sigmoid(pz))

    pab = _dot(h, _unpack(wab_ref[...]))
    sp = pab + dtb_ref[...]
    softplus = jnp.maximum(sp, 0.0) + jnp.log1p(jnp.exp(-jnp.abs(sp)))
    lane = lax.broadcasted_iota(jnp.int32, pab.shape, 1)
    gb_ref[0] = jnp.where(lane < HEADS, -jnp.exp(alog_ref[...]) * softplus, _sigmoid(pab))

    mixa_ref[0] = _as_pairs(sgabuf[...] * _dot(yabuf[...], _unpack(waout_ref[...])))


def _gdn_kernel(x_ref, delta_ref, merge_ref, gb_ref, gng_ref, wb_ref, wo_ref,
                x1_ref, state, ogbuf, mixbuf):
    ts = x_ref.shape[1]
    q_ref, k_ref, v_ref, zs_ref = _field_views(delta_ref, DELTA_FIELDS)
    mixa_ref, sgb_ref = _field_views(merge_ref, MERGE_FIELDS)

    @pl.when(pl.program_id(1) == 0)
    def _():
        state[...] = jnp.zeros(state.shape, F32)
        ogbuf[...] = jnp.zeros(ogbuf.shape, BF16)

    slot = pl.program_id(1) % 2
    og_new = ogbuf.at[slot]
    og_old = ogbuf.at[1 - slot]

    row = lax.broadcasted_iota(jnp.int32, (ts, ts), 0)
    col = lax.broadcasted_iota(jnp.int32, (ts, ts), 1)
    same = (row // CHUNK) == (col // CHUNK)
    tril = same & (col <= row)
    strict = same & (col < row)

    gbv = gb_ref[0]
    tril_b = tril.astype(BF16)
    same_b = same.astype(BF16)
    G = None
    GL = None
    for part in _split3(gbv):
        G = _dot(tril_b, part) if G is None else G + _dot(tril_b, part)
        GL = _dot(same_b, part) if GL is None else GL + _dot(same_b, part)
    GT = G.T
    eG = jnp.exp(G)
    eGLG = jnp.exp(GL - G)
    eGL = jnp.exp(GL)
    gng = gng_ref[...]

    lanes = [slice(hd * HEAD_DIM, (hd + 1) * HEAD_DIM) for hd in range(HEADS)]
    n_chunks = ts // CHUNK
    lane_chunk = lax.broadcasted_iota(jnp.int32, (CHUNK, ts), 1) // CHUNK
    eye_sbs = (lax.broadcasted_iota(jnp.int32, (CHUNK, ts), 0)
               == lax.broadcasted_iota(jnp.int32, (CHUNK, ts), 1) % CHUNK).astype(F32)

    def fold(m):
        acc = m[:CHUNK]
        for c in range(1, n_chunks):
            acc = acc + m[c * CHUNK:(c + 1) * CHUNK]
        return acc

    def unfold(m):
        return jnp.concatenate(
            [jnp.where(lane_chunk == c, m, jnp.zeros_like(m)) for c in range(n_chunks)], axis=0)

    st = [dict() for _ in range(HEADS)]

    def setup(group):
        def prep(hd):
            d = st[hd]
            q = _unpack(q_ref[0, :, lanes[hd]])
            k = _unpack(k_ref[0, :, lanes[hd]])
            v = _unpack(v_ref[0, :, lanes[hd]])
            beta = gbv[:, HEADS + hd:HEADS + hd + 1]
            eg = eG[:, hd:hd + 1]
            kf = k.astype(F32)
            kb = kf * beta
            d["q"], d["k"], d["kb"] = q, k, kb.astype(BF16)
            d["qd"] = (q.astype(F32) * eg).astype(BF16)
            d["kd"] = (kf * eGLG[:, hd:hd + 1]).astype(BF16)
            d["rhs"] = jnp.concatenate([v.astype(F32) * beta, kb * eg], axis=1).astype(BF16)
            d["decay"] = jnp.where(tril, jnp.exp(G[:, hd:hd + 1] - GT[hd:hd + 1, :]), 0.0)

        def products(hd):
            d = st[hd]
            d["L"] = jnp.where(strict, _dot_nt(d["kb"], d["k"]) * d["decay"], 0.0)
            d["A"] = (_dot_nt(d["q"], d["k"]) * d["decay"]).astype(BF16)

        return [functools.partial(f, hd) for hd in group for f in (prep, products)]

    def inverse(group):
        def first(hd):
            d = st[hd]
            l_sbs = fold(d["L"])
            d["T"] = eye_sbs - l_sbs
            d["P"] = _dot(l_sbs.astype(BF16), d["L"].astype(BF16))

        def both(hd):
            d = st[hd]
            p_b = d["P"].astype(BF16)
            tp = _dot(jnp.concatenate([d["T"].astype(BF16), p_b], axis=0), unfold(p_b))
            d["T"] = d["T"] + tp[:CHUNK]
            d["P"] = tp[CHUNK:]

        def last(hd):
            d = st[hd]
            d["T"] = d["T"] + _dot(d["T"].astype(BF16), unfold(d["P"].astype(BF16)))

        def solve(hd):
            d = st[hd]
            uw = _dot(unfold(d["T"].astype(BF16)), d["rhs"])
            d["u"] = uw[:, :HEAD_DIM]
            d["w"] = uw[:, HEAD_DIM:].astype(BF16)

        steps = [functools.partial(first, hd) for hd in group]
        n = 4
        while n < CHUNK:
            steps += [functools.partial(both, hd) for hd in group]
            n *= 2
        return steps + [functools.partial(f, hd) for f in (last, solve) for hd in group]

    def recurrence(group):
        def load(hd):
            st[hd]["S"] = state[hd]

        def apply_state(c, hd):
            d = st[hd]
            rows = slice(c * CHUNK, (c + 1) * CHUNK)
            d["wq"] = _dot(jnp.concatenate([d["w"][rows], d["qd"][rows]], axis=0), d["S"].astype(BF16))

        def update_state(c, hd):
            d = st[hd]
            rows = slice(c * CHUNK, (c + 1) * CHUNK)
            d["vn"] = (d["u"][rows] - d["wq"][:CHUNK]).astype(BF16)
            d["S"] = d["S"] * eGL[c * CHUNK:c * CHUNK + 1, hd:hd + 1] + _dot_tn(d["kd"][rows], d["vn"])

        def output(c, hd):
            d = st[hd]
            rows = slice(c * CHUNK, (c + 1) * CHUNK)
            pair_rows = slice(c * CHUNK // 2, (c + 1) * CHUNK // 2)
            o = d["wq"][CHUNK:] + _dot(d["A"][rows, rows], d["vn"])
            o = o * lax.rsqrt(jnp.mean(o * o, axis=-1, keepdims=True) + EPS) * gng
            zs = _unpack(zs_ref[0, pair_rows, lanes[hd]]).astype(F32)
            og_new[rows, lanes[hd]] = (o * zs).astype(BF16)

        def store(hd):
            state[hd] = st[hd]["S"]

        steps = [functools.partial(load, hd) for hd in group]
        for c in range(n_chunks):
            for f in (apply_state, update_state, output):
                steps += [functools.partial(f, c, hd) for hd in group]
        return steps + [functools.partial(store, hd) for hd in group]

    def merge_epilogue():
        def mix(cols):
            y_b = _dot(og_old[...], _unpack(wb_ref[:, cols]))
            m = _unpack(mixa_ref[0, :, cols]).astype(F32) + _unpack(sgb_ref[0, :, cols]).astype(F32) * y_b
            mixbuf[:, cols] = m.astype(BF16)

        def out(cols):
            x1_ref[0, :, cols] = x_ref[0, :, cols] + _dot(mixbuf[...], _unpack(wo_ref[:, cols]))

        pieces = [slice(start, start + COL_CHUNK) for start in range(0, D_MODEL, COL_CHUNK)]
        return [functools.partial(f, cols) for f in (mix, out) for cols in pieces]

    def run(steps):
        for step in steps:
            step()

    heads = list(range(HEADS))
    run(setup(heads))
    run(_merge(inverse(heads) + recurrence(heads), merge_epilogue()))


def _ffn_kernel(x1_ref, ng_ref, wup_ref, cw_ref, wdown_ref, nf_ref, out_ref, ubuf, *, final_norm):
    d_ff = 2 * wdown_ref.shape[0]

    @pl.when(pl.program_id(1) == 0)
    def _():
        ubuf[...] = jnp.zeros(ubuf.shape, F32)

    x1 = x1_ref[0]
    h = _rmsnorm(x1, ng_ref[...]).astype(BF16)
    up = _causal_conv(_dot(h, _unpack(wup_ref[...])), ubuf, cw_ref[...])
    gate = up[:, :d_ff]
    act = (gate * _sigmoid(gate) * up[:, d_ff:]).astype(BF16)
    x2 = x1 + _dot(act, _unpack(wdown_ref[...]))
    out_ref[0] = _rmsnorm(x2, nf_ref[...]) if final_norm else x2


def _const_spec(shape):
    return pl.BlockSpec(shape, lambda b, s: (0,) * len(shape), pipeline_mode=pl.Buffered(1))


def _tile_spec(width, rows):
    return pl.BlockSpec((1, rows, width), lambda b, s: (b, s, 0))


def _params():
    return pltpu.CompilerParams(dimension_semantics=("arbitrary", "arbitrary"),
                                vmem_limit_bytes=VMEM_LIMIT)


def kernel(x, norm_mix_g, w_in, conv_a_w, gdn_conv_w, gdn_A_log, gdn_dt_bias, gdn_norm_g,
           w_a_out, w_b_out, w_o, norm_ffn_g, w_up, ffn_conv_w, w_down, norm_final_g):
    B, S, D = x.shape
    depth = w_in.shape[0]
    ts = SEQ_TILE
    n_tiles = S // ts
    grid = (B, n_tiles)
    d_ff = w_down.shape[1]
    tok = jax.ShapeDtypeStruct((B, S, D), F32)
    n_delta, n_merge = len(DELTA_FIELDS) * D, len(MERGE_FIELDS) * D
    pairs = lambda width: jax.ShapeDtypeStruct((B, S // 2, width), jnp.int32)
    cur_tile = lambda b, s: (b, jnp.minimum(s, n_tiles - 1), 0)
    prev_tile = lambda b, s: (b, jnp.maximum(s - 1, 0), 0)

    for l in range(depth):
        w_in_t = jnp.swapaxes(w_in, 1, 2)
        w_main = _pack_t(w_in_t, l, n_blocks=7)
        w_ab = _pack(jnp.pad(w_in_t[l, 7 * D:7 * D + 2 * HEADS, :].T, ((0, 0), (0, LANES - 2 * HEADS))))
        w_gate = _pack_t(w_in_t[l, 7 * D + 2 * HEADS:, :])
        alog = jnp.pad(gdn_A_log[l], (0, LANES - HEADS)).reshape(1, LANES)
        dtb = jnp.pad(gdn_dt_bias[l], (0, LANES - HEADS)).reshape(1, LANES)

        delta_in, merge_in, gb = pl.pallas_call(
            _in_proj_kernel,
            grid=grid,
            in_specs=[_tile_spec(D, ts), _const_spec((1, D)),
                      _const_spec((D // 2, 7 * D)),
                      _const_spec((D // 2, LANES)), _const_spec((D // 2, 2 * D)), _const_spec((D // 2, D)),
                      _const_spec((CONV_K, D)), _const_spec((GDN_CONV_K, 3 * D)),
                      _const_spec((1, LANES)), _const_spec((1, LANES))],
            out_specs=[_tile_spec(n_delta, ts // 2), _tile_spec(n_merge, ts // 2), _tile_spec(LANES, ts)],
            out_shape=[pairs(n_delta), pairs(n_merge), jax.ShapeDtypeStruct((B, S, LANES), F32)],
            scratch_shapes=[pltpu.VMEM((CARRY_ROWS, D), F32),
                            pltpu.VMEM((CARRY_ROWS, 3 * D), F32),
                            pltpu.VMEM((ts, D), BF16), pltpu.VMEM((ts, D), F32)],
            compiler_params=_params(),
            name="in_proj",
        )(x, norm_mix_g[l].reshape(1, D), w_main, w_ab, w_gate, _pack(w_a_out, l),
          conv_a_w[l], gdn_conv_w[l], alog, dtb)

        x = pl.pallas_call(
            _gdn_kernel,
            grid=(B, n_tiles + 1),
            in_specs=[pl.BlockSpec((1, ts, D), prev_tile),
                      pl.BlockSpec((1, ts // 2, n_delta), cur_tile),
                      pl.BlockSpec((1, ts // 2, n_merge), prev_tile),
                      pl.BlockSpec((1, ts, LANES), cur_tile), _const_spec((1, HEAD_DIM)),
                _const_spec((D // 2, D)), _const_spec((D // 2, D))],
            out_specs=pl.BlockSpec((1, ts, D), prev_tile),
            out_shape=tok,
            scratch_shapes=[pltpu.VMEM((HEADS, HEAD_DIM, HEAD_DIM), F32),
                            pltpu.VMEM((2, ts, D), BF16), pltpu.VMEM((ts, D), BF16)],
            compiler_params=_params(),
            name="gdn_mix",
        )(x, delta_in, merge_in, gb, gdn_norm_g[l].reshape(1, HEAD_DIM),
          _pack(w_b_out, l), _pack(w_o, l))

        x = pl.pallas_call(
            functools.partial(_ffn_kernel, final_norm=(l == depth - 1)),
            grid=grid,
            in_specs=[_tile_spec(D, ts), _const_spec((1, D)), _const_spec((D // 2, 2 * d_ff)),
                      _const_spec((CONV_K, 2 * d_ff)), _const_spec((d_ff // 2, D)), _const_spec((1, D))],
            out_specs=_tile_spec(D, ts),
            out_shape=tok,
            scratch_shapes=[pltpu.VMEM((CARRY_ROWS, 2 * d_ff), F32)],
            compiler_params=_params(),
            name="conv_ffn",
        )(x, norm_ffn_g[l].reshape(1, D), _pack(w_up, l, block_cols=512), ffn_conv_w[l],
          _pack(w_down, l, block_rows=704), norm_final_g.reshape(1, D))
    return x
```

```python
import functools

import jax
import jax.numpy as jnp
from jax import lax
from jax.experimental import pallas as pl
from jax.experimental.pallas import tpu as pltpu

D_MODEL = 1024
HEADS = 8
HEAD_DIM = 128
GDN_CONV_K = 4
CONV_K = 3
EPS = 1e-6
LANES = 128
CARRY_ROWS = 8
SEQ_TILE = 256
CHUNK = 64
COL_CHUNK = 256
WEIGHT_BLOCK = 512
DELTA_FIELDS = ("q", "k", "v", "zs")
MERGE_FIELDS = ("mixa", "sgb")
VMEM_LIMIT = 52 * 1024 * 1024

F32 = jnp.float32
BF16 = jnp.bfloat16


def _dot(a, b):
    return jnp.dot(a, b, preferred_element_type=F32)


def _dot_nt(a, b):
    return lax.dot_general(a, b, (((1,), (1,)), ((), ())), preferred_element_type=F32)


def _dot_tn(a, b):
    return lax.dot_general(a, b, (((0,), (0,)), ((), ())), preferred_element_type=F32)


def _convert_weight(blocks, stage, sem, transpose=False):
    def copy(i):
        return pltpu.make_async_copy(blocks[i][0], stage.at[i % 2], sem.at[i % 2])

    for i in range(min(2, len(blocks))):
        copy(i).start()
    for i, (_, dst) in enumerate(blocks):
        copy(i).wait()
        block = stage[i % 2]
        dst[...] = _as_pairs(block.T if transpose else block)
        if i + 2 < len(blocks):
            copy(i + 2).start()


def _first_step():
    return (pl.program_id(0) == 0) & (pl.program_id(1) == 0)


def _unpack(x_i32):
    return pltpu.bitcast(x_i32, BF16)


def _as_pairs(x):
    return pltpu.bitcast(x.astype(BF16), jnp.int32)


def _sigmoid(x):
    return 0.5 * jnp.tanh(0.5 * x) + 0.5


def _rmsnorm(x, g):
    return x * lax.rsqrt(jnp.mean(x * x, axis=-1, keepdims=True) + EPS) * g


def _causal_conv(x, carry_ref, w):
    taps = w.shape[0]
    rows = x.shape[0]
    carry = carry_ref[...]
    sub = lax.broadcasted_iota(jnp.int32, carry.shape, 0)
    acc = w[taps - 1:taps, :] * x
    for d in range(1, taps):
        rolled = pltpu.roll(x, d, 0)
        head = jnp.where(sub < d, pltpu.roll(carry, d, 0), rolled[:CARRY_ROWS])
        shifted = jnp.concatenate([head, rolled[CARRY_ROWS:]], axis=0)
        acc = acc + w[taps - 1 - d:taps - d, :] * shifted
    carry_ref[...] = x[rows - CARRY_ROWS:, :]
    return acc


def _field_views(ref, fields):
    return [ref.at[:, :, i * D_MODEL:(i + 1) * D_MODEL] for i in range(len(fields))]


def _merge(a, b):
    out, i, j = [], 0, 0
    while i < len(a) or j < len(b):
        if j >= len(b) or (i < len(a) and i * len(b) <= j * len(a)):
            out.append(a[i])
            i += 1
        else:
            out.append(b[j])
            j += 1
    return out


def _split3(x):
    hi = x.astype(BF16)
    r = x - hi.astype(F32)
    mid = r.astype(BF16)
    lo = (r - mid.astype(F32)).astype(BF16)
    return hi, mid, lo


def _in_proj_kernel(x_ref, ng_ref, win_hbm, waout_hbm,
                    cwa_ref, cwg_ref, alog_ref, dtb_ref,
                    delta_ref, merge_ref, gb_ref,
                    cbuf, qbuf, yabuf, sgabuf, wmain_ref, wab_ref, wgate_ref, waout_ref, stage, sem,
                    *, layer):
    q_ref, k_ref, v_ref, zs_ref = _field_views(delta_ref, DELTA_FIELDS)
    mixa_ref, sgb_ref = _field_views(merge_ref, MERGE_FIELDS)

    @pl.when(_first_step())
    def _():
        rows = stage.shape[1]
        n_main, gate0 = wmain_ref.shape[1], wmain_ref.shape[1] + 2 * HEADS
        _convert_weight([(win_hbm.at[layer, r:r + rows, :], wmain_ref.at[:, r:r + rows])
                         for r in range(0, n_main, rows)], stage, sem, transpose=True)
        _convert_weight([(win_hbm.at[layer, gate0 + r:gate0 + r + rows, :], wgate_ref.at[:, r:r + rows])
                         for r in range(0, wgate_ref.shape[1], rows)], stage, sem, transpose=True)
        _convert_weight([(waout_hbm.at[layer, r:r + rows, :], waout_ref.at[r // 2:(r + rows) // 2, :])
                         for r in range(0, D_MODEL, rows)], stage, sem)
        ab_copy = pltpu.make_async_copy(win_hbm.at[layer, n_main:gate0, :], stage.at[0, 0:2 * HEADS, :],
                                        sem.at[0])
        ab_copy.start()
        ab_copy.wait()
        row = lax.broadcasted_iota(jnp.int32, (LANES, D_MODEL), 0)
        wab_ref[...] = _as_pairs(jnp.where(row < 2 * HEADS, stage[0, 0:LANES, :], 0.0).T)

    @pl.when(pl.program_id(1) == 0)
    def _():
        cbuf[...] = jnp.zeros(cbuf.shape, F32)
        qbuf[...] = jnp.zeros(qbuf.shape, F32)

    h = _rmsnorm(x_ref[0], ng_ref[...]).astype(BF16)

    def proj(block, cols):
        start = block * D_MODEL + cols.start
        return _dot(h, _unpack(wmain_ref[:, start:start + COL_CHUNK]))

    def conv_silu(group, cols):
        gcols = slice(group * D_MODEL + cols.start, group * D_MODEL + cols.stop)
        c = _causal_conv(proj(3 + group, cols), qbuf.at[:, gcols], cwg_ref[:, gcols])
        return c * _sigmoid(c)

    def l2norm_heads(c, scale, out_ref, cols):
        for lo in range(0, COL_CHUNK, HEAD_DIM):
            ch = c[:, lo:lo + HEAD_DIM]
            inv = lax.rsqrt(jnp.sum(ch * ch, axis=-1, keepdims=True) + EPS)
            out_ref[0, :, cols.start + lo:cols.start + lo + HEAD_DIM] = _as_pairs(
                ch * (inv * scale if scale != 1.0 else inv))

    for start in range(0, D_MODEL, COL_CHUNK):
        cols = slice(start, start + COL_CHUNK)
        l2norm_heads(conv_silu(0, cols), HEAD_DIM ** -0.5, q_ref, cols)
        conv = _causal_conv(proj(1, cols) * proj(2, cols), cbuf.at[:, cols], cwa_ref[:, cols])
        l2norm_heads(conv_silu(1, cols), 1.0, k_ref, cols)
        yabuf[:, cols] = (proj(0, cols) * conv).astype(BF16)
        sgabuf[:, cols] = _sigmoid(_dot(h, _unpack(wgate_ref[:, cols])))
        v_ref[0, :, cols] = _as_pairs(conv_silu(2, cols))
        sgb_ref[0, :, cols] = _as_pairs(_sigmoid(
            _dot(h, _unpack(wgate_ref[:, D_MODEL + start:D_MODEL + start + COL_CHUNK]))))
        pz = proj(6, cols)
        zs_ref[0, :, cols] = _as_pairs(pz * _sigmoid(pz))

    pab = _dot(h, _unpack(wab_ref[...]))
    sp = pab + dtb_ref[...]
    softplus = jnp.maximum(sp, 0.0) + jnp.log1p(jnp.exp(-jnp.abs(sp)))
    lane = lax.broadcasted_iota(jnp.int32, pab.shape, 1)
    gb_ref[0] = jnp.where(lane < HEADS, -jnp.exp(alog_ref[...]) * softplus, _sigmoid(pab))

    mixa_ref[0] = _as_pairs(sgabuf[...] * _dot(yabuf[...], _unpack(waout_ref[...])))


def _gdn_kernel(x_ref, delta_ref, merge_ref, gb_ref, gng_ref, wb_hbm, wo_hbm,
                x1_ref, state, ogbuf, mixbuf, wb_ref, wo_ref, stage, sem, *, layer):
    ts = x_ref.shape[1]
    q_ref, k_ref, v_ref, zs_ref = _field_views(delta_ref, DELTA_FIELDS)
    mixa_ref, sgb_ref = _field_views(merge_ref, MERGE_FIELDS)

    @pl.when(_first_step())
    def _():
        rows = stage.shape[1]
        for w_hbm, w_ref in ((wb_hbm, wb_ref), (wo_hbm, wo_ref)):
            _convert_weight([(w_hbm.at[layer, r:r + rows, :], w_ref.at[r // 2:(r + rows) // 2, :])
                             for r in range(0, D_MODEL, rows)], stage, sem)

    @pl.when(pl.program_id(1) == 0)
    def _():
        state[...] = jnp.zeros(state.shape, F32)
        ogbuf[...] = jnp.zeros(ogbuf.shape, BF16)

    slot = pl.program_id(1) % 2
    og_new = ogbuf.at[slot]
    og_old = ogbuf.at[1 - slot]

    row = lax.broadcasted_iota(jnp.int32, (ts, ts), 0)
    col = lax.broadcasted_iota(jnp.int32, (ts, ts), 1)
    same = (row // CHUNK) == (col // CHUNK)
    tril = same & (col <= row)
    strict = same & (col < row)

    gbv = gb_ref[0]
    tril_b = tril.astype(BF16)
    same_b = same.astype(BF16)
    G = None
    GL = None
    for part in _split3(gbv):
        G = _dot(tril_b, part) if G is None else G + _dot(tril_b, part)
        GL = _dot(same_b, part) if GL is None else GL + _dot(same_b, part)
    GT = G.T
    eG = jnp.exp(G)
    eGLG = jnp.exp(GL - G)
    eGL = jnp.exp(GL)
    gng = gng_ref[...]

    lanes = [slice(hd * HEAD_DIM, (hd + 1) * HEAD_DIM) for hd in range(HEADS)]
    n_chunks = ts // CHUNK
    lane_chunk = lax.broadcasted_iota(jnp.int32, (CHUNK, ts), 1) // CHUNK
    eye_sbs = (lax.broadcasted_iota(jnp.int32, (CHUNK, ts), 0)
               == lax.broadcasted_iota(jnp.int32, (CHUNK, ts), 1) % CHUNK).astype(F32)

    def fold(m):
        acc = m[:CHUNK]
        for c in range(1, n_chunks):
            acc = acc + m[c * CHUNK:(c + 1) * CHUNK]
        return acc

    def unfold(m):
        return jnp.concatenate(
            [jnp.where(lane_chunk == c, m, jnp.zeros_like(m)) for c in range(n_chunks)], axis=0)

    st = [dict() for _ in range(HEADS)]

    def setup(group):
        def prep(hd):
            d = st[hd]
            q = _unpack(q_ref[0, :, lanes[hd]])
            k = _unpack(k_ref[0, :, lanes[hd]])
            v = _unpack(v_ref[0, :, lanes[hd]])
            beta = gbv[:, HEADS + hd:HEADS + hd + 1]
            eg = eG[:, hd:hd + 1]
            kf = k.astype(F32)
            kb = kf * beta
            d["q"], d["k"], d["kb"] = q, k, kb.astype(BF16)
            d["qd"] = (q.astype(F32) * eg).astype(BF16)
            d["kd"] = (kf * eGLG[:, hd:hd + 1]).astype(BF16)
            d["rhs"] = jnp.concatenate([v.astype(F32) * beta, kb * eg], axis=1).astype(BF16)
            d["decay"] = jnp.where(tril, jnp.exp(G[:, hd:hd + 1] - GT[hd:hd + 1, :]), 0.0)

        def products(hd):
            d = st[hd]
            d["L"] = jnp.where(strict, _dot_nt(d["kb"], d["k"]) * d["decay"], 0.0)
            d["A"] = (_dot_nt(d["q"], d["k"]) * d["decay"]).astype(BF16)

        return [functools.partial(f, hd) for hd in group for f in (prep, products)]

    def inverse(group):
        def first(hd):
            d = st[hd]
            l_sbs = fold(d["L"])
            d["T"] = eye_sbs - l_sbs
            d["P"] = _dot(l_sbs.astype(BF16), d["L"].astype(BF16))

        def both(hd):
            d = st[hd]
            p_b = d["P"].astype(BF16)
            tp = _dot(jnp.concatenate([d["T"].astype(BF16), p_b], axis=0), unfold(p_b))
            d["T"] = d["T"] + tp[:CHUNK]
            d["P"] = tp[CHUNK:]

        def last(hd):
            d = st[hd]
            d["T"] = d["T"] + _dot(d["T"].astype(BF16), unfold(d["P"].astype(BF16)))

        def solve(hd):
            d = st[hd]
            uw = _dot(unfold(d["T"].astype(BF16)), d["rhs"])
            d["u"] = uw[:, :HEAD_DIM]
            d["w"] = uw[:, HEAD_DIM:].astype(BF16)

        steps = [functools.partial(first, hd) for hd in group]
        n = 4
        while n < CHUNK:
            steps += [functools.partial(both, hd) for hd in group]
            n *= 2
        return steps + [functools.partial(f, hd) for f in (last, solve) for hd in group]

    def recurrence(group):
        def load(hd):
            st[hd]["S"] = state[hd]

        def apply_state(c, hd):
            d = st[hd]
            rows = slice(c * CHUNK, (c + 1) * CHUNK)
            d["wq"] = _dot(jnp.concatenate([d["w"][rows], d["qd"][rows]], axis=0), d["S"].astype(BF16))

        def update_state(c, hd):
            d = st[hd]
            rows = slice(c * CHUNK, (c + 1) * CHUNK)
            d["vn"] = (d["u"][rows] - d["wq"][:CHUNK]).astype(BF16)
            d["S"] = d["S"] * eGL[c * CHUNK:c * CHUNK + 1, hd:hd + 1] + _dot_tn(d["kd"][rows], d["vn"])

        def output(c, hd):
            d = st[hd]
            rows = slice(c * CHUNK, (c + 1) * CHUNK)
            pair_rows = slice(c * CHUNK // 2, (c + 1) * CHUNK // 2)
            o = d["wq"][CHUNK:] + _dot(d["A"][rows, rows], d["vn"])
            o = o * lax.rsqrt(jnp.mean(o * o, axis=-1, keepdims=True) + EPS) * gng
            zs = _unpack(zs_ref[0, pair_rows, lanes[hd]]).astype(F32)
            og_new[rows, lanes[hd]] = (o * zs).astype(BF16)

        def store(hd):
            state[hd] = st[hd]["S"]

        steps = [functools.partial(load, hd) for hd in group]
        for c in range(n_chunks):
            for f in (apply_state, update_state, output):
                steps += [functools.partial(f, c, hd) for hd in group]
        return steps + [functools.partial(store, hd) for hd in group]

    def merge_epilogue():
        def mix(cols):
            y_b = _dot(og_old[...], _unpack(wb_ref[:, cols]))
            m = _unpack(mixa_ref[0, :, cols]).astype(F32) + _unpack(sgb_ref[0, :, cols]).astype(F32) * y_b
            mixbuf[:, cols] = m.astype(BF16)

        def out(cols):
            x1_ref[0, :, cols] = x_ref[0, :, cols] + _dot(mixbuf[...], _unpack(wo_ref[:, cols]))

        pieces = [slice(start, start + COL_CHUNK) for start in range(0, D_MODEL, COL_CHUNK)]
        return [functools.partial(f, cols) for f in (mix, out) for cols in pieces]

    def run(steps):
        for step in steps:
            step()

    heads = list(range(HEADS))
    run(setup(heads))
    run(_merge(inverse(heads) + recurrence(heads), merge_epilogue()))


def _ffn_kernel(x1_ref, ng_ref, wup_hbm, cw_ref, wdown_hbm, nf_ref, out_ref,
                ubuf, wup_ref, wdown_ref, stage_up, stage_down, sem, *, layer, final_norm):
    d_ff = 2 * wdown_ref.shape[0]

    @pl.when(_first_step())
    def _():
        cols, rows = stage_up.shape[2], stage_down.shape[1]
        _convert_weight([(wup_hbm.at[layer, :, c:c + cols], wup_ref.at[:, c:c + cols])
                         for c in range(0, 2 * d_ff, cols)], stage_up, sem)
        _convert_weight([(wdown_hbm.at[layer, r:r + rows, :], wdown_ref.at[r // 2:(r + rows) // 2, :])
                         for r in range(0, d_ff, rows)], stage_down, sem)

    @pl.when(pl.program_id(1) == 0)
    def _():
        ubuf[...] = jnp.zeros(ubuf.shape, F32)

    x1 = x1_ref[0]
    h = _rmsnorm(x1, ng_ref[...]).astype(BF16)
    up = _causal_conv(_dot(h, _unpack(wup_ref[...])), ubuf, cw_ref[...])
    gate = up[:, :d_ff]
    act = (gate * _sigmoid(gate) * up[:, d_ff:]).astype(BF16)
    x2 = x1 + _dot(act, _unpack(wdown_ref[...]))
    out_ref[0] = _rmsnorm(x2, nf_ref[...]) if final_norm else x2


def _const_spec(shape):
    return pl.BlockSpec(shape, lambda b, s: (0,) * len(shape), pipeline_mode=pl.Buffered(1))


def _tile_spec(width, rows):
    return pl.BlockSpec((1, rows, width), lambda b, s: (b, s, 0))


def _params():
    return pltpu.CompilerParams(dimension_semantics=("arbitrary", "arbitrary"),
                                vmem_limit_bytes=VMEM_LIMIT)


def kernel(x, norm_mix_g, w_in, conv_a_w, gdn_conv_w, gdn_A_log, gdn_dt_bias, gdn_norm_g,
           w_a_out, w_b_out, w_o, norm_ffn_g, w_up, ffn_conv_w, w_down, norm_final_g):
    B, S, D = x.shape
    depth = w_in.shape[0]
    ts = SEQ_TILE
    n_tiles = S // ts
    grid = (B, n_tiles)
    d_ff = w_down.shape[1]
    tok = jax.ShapeDtypeStruct((B, S, D), F32)
    n_delta, n_merge = len(DELTA_FIELDS) * D, len(MERGE_FIELDS) * D
    pairs = lambda width: jax.ShapeDtypeStruct((B, S // 2, width), jnp.int32)
    cur_tile = lambda b, s: (b, jnp.minimum(s, n_tiles - 1), 0)
    prev_tile = lambda b, s: (b, jnp.maximum(s - 1, 0), 0)
    hbm = pl.BlockSpec(memory_space=pl.ANY)
    pairs_vmem = lambda k, n: pltpu.VMEM((k // 2, n), jnp.int32)

    for l in range(depth):
        w_in_t = jnp.swapaxes(w_in, 1, 2)
        alog = jnp.pad(gdn_A_log[l], (0, LANES - HEADS)).reshape(1, LANES)
        dtb = jnp.pad(gdn_dt_bias[l], (0, LANES - HEADS)).reshape(1, LANES)

        delta_in, merge_in, gb = pl.pallas_call(
            functools.partial(_in_proj_kernel, layer=l),
            grid=grid,
            in_specs=[_tile_spec(D, ts), _const_spec((1, D)), hbm, hbm,
                      _const_spec((CONV_K, D)), _const_spec((GDN_CONV_K, 3 * D)),
                      _const_spec((1, LANES)), _const_spec((1, LANES))],
            out_specs=[_tile_spec(n_delta, ts // 2), _tile_spec(n_merge, ts // 2), _tile_spec(LANES, ts)],
            out_shape=[pairs(n_delta), pairs(n_merge), jax.ShapeDtypeStruct((B, S, LANES), F32)],
            scratch_shapes=[pltpu.VMEM((CARRY_ROWS, D), F32),
                            pltpu.VMEM((CARRY_ROWS, 3 * D), F32),
                            pltpu.VMEM((ts, D), BF16), pltpu.VMEM((ts, D), F32),
                            pairs_vmem(D, 7 * D), pairs_vmem(D, LANES), pairs_vmem(D, 2 * D),
                            pairs_vmem(D, D), pltpu.VMEM((2, WEIGHT_BLOCK, D), F32),
                            pltpu.SemaphoreType.DMA((2,))],
            compiler_params=_params(),
            name="in_proj",
        )(x, norm_mix_g[l].reshape(1, D), w_in_t, w_a_out,
          conv_a_w[l], gdn_conv_w[l], alog, dtb)

        x = pl.pallas_call(
            functools.partial(_gdn_kernel, layer=l),
            grid=(B, n_tiles + 1),
            in_specs=[pl.BlockSpec((1, ts, D), prev_tile),
                      pl.BlockSpec((1, ts // 2, n_delta), cur_tile),
                      pl.BlockSpec((1, ts // 2, n_merge), prev_tile),
                      pl.BlockSpec((1, ts, LANES), cur_tile), _const_spec((1, HEAD_DIM)), hbm, hbm],
            out_specs=pl.BlockSpec((1, ts, D), prev_tile),
            out_shape=tok,
            scratch_shapes=[pltpu.VMEM((HEADS, HEAD_DIM, HEAD_DIM), F32),
                            pltpu.VMEM((2, ts, D), BF16), pltpu.VMEM((ts, D), BF16),
                            pairs_vmem(D, D), pairs_vmem(D, D),
                            pltpu.VMEM((2, WEIGHT_BLOCK, D), F32), pltpu.SemaphoreType.DMA((2,))],
            compiler_params=_params(),
            name="gdn_mix",
        )(x, delta_in, merge_in, gb, gdn_norm_g[l].reshape(1, HEAD_DIM), w_b_out, w_o)

        x = pl.pallas_call(
            functools.partial(_ffn_kernel, layer=l, final_norm=(l == depth - 1)),
            grid=grid,
            in_specs=[_tile_spec(D, ts), _const_spec((1, D)), hbm,
                      _const_spec((CONV_K, 2 * d_ff)), hbm, _const_spec((1, D))],
            out_specs=_tile_spec(D, ts),
            out_shape=tok,
            scratch_shapes=[pltpu.VMEM((CARRY_ROWS, 2 * d_ff), F32),
                            pairs_vmem(D, 2 * d_ff), pairs_vmem(d_ff, D),
                            pltpu.VMEM((2, D, COL_CHUNK), F32), pltpu.VMEM((2, d_ff // 8, D), F32),
                            pltpu.SemaphoreType.DMA((2,))],
            compiler_params=_params(),
            name="conv_ffn",
        )(x, norm_ffn_g[l].reshape(1, D), w_up, ffn_conv_w[l], w_down, norm_final_g.reshape(1, D))
    return x
```

```python
import functools

import jax
import jax.numpy as jnp
from jax import lax
from jax.experimental import pallas as pl
from jax.experimental.pallas import tpu as pltpu

D_MODEL = 1024
HEADS = 8
HEAD_DIM = 128
GDN_CONV_K = 4
CONV_K = 3
EPS = 1e-6
LANES = 128
CARRY_ROWS = 8
SEQ_TILE = 256
CHUNK = 64
COL_CHUNK = 256
WEIGHT_BLOCK = 512
WEIGHT_SLOTS = 3
DELTA_FIELDS = ("q", "k", "v", "zs")
MERGE_FIELDS = ("mixa", "sgb")
VMEM_LIMIT = 52 * 1024 * 1024

F32 = jnp.float32
BF16 = jnp.bfloat16


def _dot(a, b):
    return jnp.dot(a, b, preferred_element_type=F32)


def _dot_nt(a, b):
    return lax.dot_general(a, b, (((1,), (1,)), ((), ())), preferred_element_type=F32)


def _dot_tn(a, b):
    return lax.dot_general(a, b, (((0,), (0,)), ((), ())), preferred_element_type=F32)


def _convert_weight(blocks, stage, sem, transpose=False):
    slots = stage.shape[0]

    def copy(i):
        return pltpu.make_async_copy(blocks[i][0], stage.at[i % slots], sem.at[i % slots])

    for i in range(min(slots, len(blocks))):
        copy(i).start()
    for i, (_, dst) in enumerate(blocks):
        copy(i).wait()
        block = stage[i % slots]
        dst[...] = _as_pairs(block.T if transpose else block)
        if i + slots < len(blocks):
            copy(i + slots).start()


def _first_step():
    return (pl.program_id(0) == 0) & (pl.program_id(1) == 0)


def _unpack(x_i32):
    return pltpu.bitcast(x_i32, BF16)


def _as_pairs(x):
    return pltpu.bitcast(x.astype(BF16), jnp.int32)


def _sigmoid(x):
    return 0.5 * jnp.tanh(0.5 * x) + 0.5


def _silu(x):
    h = 0.5 * x
    return h * jnp.tanh(h) + h


def _rmsnorm(x, g):
    return x * lax.rsqrt(jnp.mean(x * x, axis=-1, keepdims=True) + EPS) * g


def _causal_conv(x, carry_ref, w):
    taps = w.shape[0]
    rows = x.shape[0]
    tiles = [carry_ref[...]] + [x[r:r + CARRY_ROWS, :] for r in range(0, rows, CARRY_ROWS)]
    sub = lax.broadcasted_iota(jnp.int32, tiles[0].shape, 0)
    acc = w[taps - 1:taps, :] * x
    for d in range(1, taps):
        rolled = [pltpu.roll(tile, d, 0) for tile in tiles]
        shifted = jnp.concatenate(
            [jnp.where(sub < d, before, here) for before, here in zip(rolled[:-1], rolled[1:])], axis=0)
        acc = acc + w[taps - 1 - d:taps - d, :] * shifted
    carry_ref[...] = tiles[-1]
    return acc


def _field_views(ref, fields):
    return [ref.at[:, :, i * D_MODEL:(i + 1) * D_MODEL] for i in range(len(fields))]


def _merge(a, b):
    out, i, j = [], 0, 0
    while i < len(a) or j < len(b):
        if j >= len(b) or (i < len(a) and i * len(b) <= j * len(a)):
            out.append(a[i])
            i += 1
        else:
            out.append(b[j])
            j += 1
    return out


def _split3(x):
    hi = x.astype(BF16)
    r = x - hi.astype(F32)
    mid = r.astype(BF16)
    lo = (r - mid.astype(F32)).astype(BF16)
    return hi, mid, lo


def _in_proj_kernel(x_ref, ng_ref, win_hbm, waout_hbm,
                    cwa_ref, cwg_ref, alog_ref, dtb_ref,
                    delta_ref, merge_ref, gb_ref,
                    cbuf, qbuf, yabuf, sgabuf, wmain_ref, wab_ref, wgate_ref, waout_ref, stage, sem,
                    *, layer):
    q_ref, k_ref, v_ref, zs_ref = _field_views(delta_ref, DELTA_FIELDS)
    mixa_ref, sgb_ref = _field_views(merge_ref, MERGE_FIELDS)

    @pl.when(_first_step())
    def _():
        rows = stage.shape[1]
        n_main, gate0 = wmain_ref.shape[1], wmain_ref.shape[1] + 2 * HEADS
        _convert_weight([(win_hbm.at[layer, r:r + rows, :], wmain_ref.at[:, r:r + rows])
                         for r in range(0, n_main, rows)], stage, sem, transpose=True)
        _convert_weight([(win_hbm.at[layer, gate0 + r:gate0 + r + rows, :], wgate_ref.at[:, r:r + rows])
                         for r in range(0, wgate_ref.shape[1], rows)], stage, sem, transpose=True)
        _convert_weight([(waout_hbm.at[layer, r:r + rows, :], waout_ref.at[r // 2:(r + rows) // 2, :])
                         for r in range(0, D_MODEL, rows)], stage, sem)
        ab_copy = pltpu.make_async_copy(win_hbm.at[layer, n_main:gate0, :], stage.at[0, 0:2 * HEADS, :],
                                        sem.at[0])
        ab_copy.start()
        ab_copy.wait()
        row = lax.broadcasted_iota(jnp.int32, (LANES, D_MODEL), 0)
        wab_ref[...] = _as_pairs(jnp.where(row < 2 * HEADS, stage[0, 0:LANES, :], 0.0).T)

    @pl.when(pl.program_id(1) == 0)
    def _():
        cbuf[...] = jnp.zeros(cbuf.shape, F32)
        qbuf[...] = jnp.zeros(qbuf.shape, F32)

    h = _rmsnorm(x_ref[0], ng_ref[...]).astype(BF16)

    def proj(block, cols):
        start = block * D_MODEL + cols.start
        return _dot(h, _unpack(wmain_ref[:, start:start + COL_CHUNK]))

    def conv_silu(group, cols):
        gcols = slice(group * D_MODEL + cols.start, group * D_MODEL + cols.stop)
        c = _causal_conv(proj(3 + group, cols), qbuf.at[:, gcols], cwg_ref[:, gcols])
        return _silu(c)

    def l2norm_heads(c, scale, out_ref, cols):
        for lo in range(0, COL_CHUNK, HEAD_DIM):
            ch = c[:, lo:lo + HEAD_DIM]
            inv = lax.rsqrt(jnp.sum(ch * ch, axis=-1, keepdims=True) + EPS)
            out_ref[0, :, cols.start + lo:cols.start + lo + HEAD_DIM] = _as_pairs(
                ch * (inv * scale if scale != 1.0 else inv))

    for start in range(0, D_MODEL, COL_CHUNK):
        cols = slice(start, start + COL_CHUNK)
        l2norm_heads(conv_silu(0, cols), HEAD_DIM ** -0.5, q_ref, cols)
        conv = _causal_conv(proj(1, cols) * proj(2, cols), cbuf.at[:, cols], cwa_ref[:, cols])
        l2norm_heads(conv_silu(1, cols), 1.0, k_ref, cols)
        yabuf[:, cols] = (proj(0, cols) * conv).astype(BF16)
        sgabuf[:, cols] = _sigmoid(_dot(h, _unpack(wgate_ref[:, cols])))
        v_ref[0, :, cols] = _as_pairs(conv_silu(2, cols))
        sgb_ref[0, :, cols] = _as_pairs(_sigmoid(
            _dot(h, _unpack(wgate_ref[:, D_MODEL + start:D_MODEL + start + COL_CHUNK]))))
        pz = proj(6, cols)
        zs_ref[0, :, cols] = _as_pairs(_silu(pz))

    pab = _dot(h, _unpack(wab_ref[...]))
    sp = pab + dtb_ref[...]
    softplus = jnp.maximum(sp, 0.0) + jnp.log1p(jnp.exp(-jnp.abs(sp)))
    lane = lax.broadcasted_iota(jnp.int32, pab.shape, 1)
    gb_ref[0] = jnp.where(lane < HEADS, -jnp.exp(alog_ref[...]) * softplus, _sigmoid(pab))

    mixa_ref[0] = _as_pairs(sgabuf[...] * _dot(yabuf[...], _unpack(waout_ref[...])))


def _gdn_kernel(x_ref, delta_ref, merge_ref, gb_ref, gng_ref, wb_hbm, wo_hbm,
                x1_ref, state, ogbuf, mixbuf, wb_ref, wo_ref, stage, sem, *, layer):
    ts = x_ref.shape[1]
    q_ref, k_ref, v_ref, zs_ref = _field_views(delta_ref, DELTA_FIELDS)
    mixa_ref, sgb_ref = _field_views(merge_ref, MERGE_FIELDS)

    @pl.when(_first_step())
    def _():
        rows = stage.shape[1]
        for w_hbm, w_ref in ((wb_hbm, wb_ref), (wo_hbm, wo_ref)):
            _convert_weight([(w_hbm.at[layer, r:r + rows, :], w_ref.at[r // 2:(r + rows) // 2, :])
                             for r in range(0, D_MODEL, rows)], stage, sem)

    @pl.when(pl.program_id(1) == 0)
    def _():
        state[...] = jnp.zeros(state.shape, F32)
        ogbuf[...] = jnp.zeros(ogbuf.shape, BF16)

    slot = pl.program_id(1) % 2
    og_new = ogbuf.at[slot]
    og_old = ogbuf.at[1 - slot]

    row = lax.broadcasted_iota(jnp.int32, (ts, ts), 0)
    col = lax.broadcasted_iota(jnp.int32, (ts, ts), 1)
    same = (row // CHUNK) == (col // CHUNK)
    tril = same & (col <= row)
    strict = same & (col < row)

    gbv = gb_ref[0]
    tril_b = tril.astype(BF16)
    same_b = same.astype(BF16)
    G = None
    GL = None
    for part in _split3(gbv):
        G = _dot(tril_b, part) if G is None else G + _dot(tril_b, part)
        GL = _dot(same_b, part) if GL is None else GL + _dot(same_b, part)
    GT = G.T
    eG = jnp.exp(G)
    eGLG = jnp.exp(GL - G)
    eGL = jnp.exp(GL)
    gng = gng_ref[...]

    lanes = [slice(hd * HEAD_DIM, (hd + 1) * HEAD_DIM) for hd in range(HEADS)]
    n_chunks = ts // CHUNK
    lane_chunk = lax.broadcasted_iota(jnp.int32, (CHUNK, ts), 1) // CHUNK
    eye_sbs = (lax.broadcasted_iota(jnp.int32, (CHUNK, ts), 0)
               == lax.broadcasted_iota(jnp.int32, (CHUNK, ts), 1) % CHUNK).astype(F32)

    def fold(m):
        acc = m[:CHUNK]
        for c in range(1, n_chunks):
            acc = acc + m[c * CHUNK:(c + 1) * CHUNK]
        return acc

    def unfold(m):
        return jnp.concatenate(
            [jnp.where(lane_chunk == c, m, jnp.zeros_like(m)) for c in range(n_chunks)], axis=0)

    st = [dict() for _ in range(HEADS)]

    def setup(group):
        def prep(hd):
            d = st[hd]
            q = _unpack(q_ref[0, :, lanes[hd]])
            k = _unpack(k_ref[0, :, lanes[hd]])
            v = _unpack(v_ref[0, :, lanes[hd]])
            beta = gbv[:, HEADS + hd:HEADS + hd + 1]
            eg = eG[:, hd:hd + 1]
            kf = k.astype(F32)
            kb = kf * beta
            d["q"], d["k"], d["kb"] = q, k, kb.astype(BF16)
            d["qd"] = (q.astype(F32) * eg).astype(BF16)
            d["kd"] = (kf * eGLG[:, hd:hd + 1]).astype(BF16)
            d["rhs"] = jnp.concatenate([v.astype(F32) * beta, kb * eg], axis=1).astype(BF16)
            d["decay"] = jnp.where(tril, jnp.exp(G[:, hd:hd + 1] - GT[hd:hd + 1, :]), 0.0)

        def products(hd):
            d = st[hd]
            d["L"] = jnp.where(strict, _dot_nt(d["kb"], d["k"]) * d["decay"], 0.0)
            d["A"] = (_dot_nt(d["q"], d["k"]) * d["decay"]).astype(BF16)

        return [functools.partial(f, hd) for hd in group for f in (prep, products)]

    def inverse(group):
        def first(hd):
            d = st[hd]
            l_sbs = fold(d["L"])
            d["T"] = eye_sbs - l_sbs
            d["P"] = _dot(l_sbs.astype(BF16), d["L"].astype(BF16))

        def both(hd):
            d = st[hd]
            p_b = d["P"].astype(BF16)
            tp = _dot(jnp.concatenate([d["T"].astype(BF16), p_b], axis=0), unfold(p_b))
            d["T"] = d["T"] + tp[:CHUNK]
            d["P"] = tp[CHUNK:]

        def last(hd):
            d = st[hd]
            d["T"] = d["T"] + _dot(d["T"].astype(BF16), unfold(d["P"].astype(BF16)))

        def solve(hd):
            d = st[hd]
            uw = _dot(unfold(d["T"].astype(BF16)), d["rhs"])
            d["u"] = uw[:, :HEAD_DIM]
            d["w"] = uw[:, HEAD_DIM:].astype(BF16)

        steps = [functools.partial(first, hd) for hd in group]
        n = 4
        while n < CHUNK:
            steps += [functools.partial(both, hd) for hd in group]
            n *= 2
        return steps + [functools.partial(f, hd) for f in (last, solve) for hd in group]

    def recurrence(group):
        def load(hd):
            st[hd]["S"] = state[hd]

        def apply_state(c, hd):
            d = st[hd]
            rows = slice(c * CHUNK, (c + 1) * CHUNK)
            d["wq"] = _dot(jnp.concatenate([d["w"][rows], d["qd"][rows]], axis=0), d["S"].astype(BF16))

        def update_state(c, hd):
            d = st[hd]
            rows = slice(c * CHUNK, (c + 1) * CHUNK)
            d["vn"] = (d["u"][rows] - d["wq"][:CHUNK]).astype(BF16)
            d["S"] = d["S"] * eGL[c * CHUNK:c * CHUNK + 1, hd:hd + 1] + _dot_tn(d["kd"][rows], d["vn"])

        def output(c, hd):
            d = st[hd]
            rows = slice(c * CHUNK, (c + 1) * CHUNK)
            pair_rows = slice(c * CHUNK // 2, (c + 1) * CHUNK // 2)
            o = d["wq"][CHUNK:] + _dot(d["A"][rows, rows], d["vn"])
            o = o * lax.rsqrt(jnp.mean(o * o, axis=-1, keepdims=True) + EPS) * gng
            zs = _unpack(zs_ref[0, pair_rows, lanes[hd]]).astype(F32)
            og_new[rows, lanes[hd]] = (o * zs).astype(BF16)

        def store(hd):
            state[hd] = st[hd]["S"]

        steps = [functools.partial(load, hd) for hd in group]
        for c in range(n_chunks):
            for f in (apply_state, update_state, output):
                steps += [functools.partial(f, c, hd) for hd in group]
        return steps + [functools.partial(store, hd) for hd in group]

    def merge_epilogue():
        def mix(cols):
            y_b = _dot(og_old[...], _unpack(wb_ref[:, cols]))
            m = _unpack(mixa_ref[0, :, cols]).astype(F32) + _unpack(sgb_ref[0, :, cols]).astype(F32) * y_b
            mixbuf[:, cols] = m.astype(BF16)

        def out(cols):
            x1_ref[0, :, cols] = x_ref[0, :, cols] + _dot(mixbuf[...], _unpack(wo_ref[:, cols]))

        pieces = [slice(start, start + COL_CHUNK) for start in range(0, D_MODEL, COL_CHUNK)]
        return [functools.partial(f, cols) for f in (mix, out) for cols in pieces]

    def run(steps):
        for step in steps:
            step()

    heads = list(range(HEADS))
    run(setup(heads))
    run(_merge(inverse(heads) + recurrence(heads), merge_epilogue()))


def _ffn_kernel(x1_ref, ng_ref, wup_hbm, cw_ref, wdown_hbm, nf_ref, out_ref,
                ubuf, wup_ref, wdown_ref, stage_up, stage_down, sem, *, layer, final_norm):
    d_ff = 2 * wdown_ref.shape[0]

    @pl.when(_first_step())
    def _():
        for w_hbm, w_ref, stage in ((wup_hbm, wup_ref, stage_up), (wdown_hbm, wdown_ref, stage_down)):
            rows = stage.shape[1]
            _convert_weight([(w_hbm.at[layer, r:r + rows, :], w_ref.at[r // 2:(r + rows) // 2, :])
                             for r in range(0, 2 * w_ref.shape[0], rows)], stage, sem)

    @pl.when(pl.program_id(1) == 0)
    def _():
        ubuf[...] = jnp.zeros(ubuf.shape, F32)

    x1 = x1_ref[0]
    h = _rmsnorm(x1, ng_ref[...]).astype(BF16)
    up = _causal_conv(_dot(h, _unpack(wup_ref[...])), ubuf, cw_ref[...])
    act = (_silu(up[:, :d_ff]) * up[:, d_ff:]).astype(BF16)
    x2 = x1 + _dot(act, _unpack(wdown_ref[...]))
    out_ref[0] = _rmsnorm(x2, nf_ref[...]) if final_norm else x2


def _const_spec(shape):
    return pl.BlockSpec(shape, lambda b, s: (0,) * len(shape), pipeline_mode=pl.Buffered(1))


def _tile_spec(width, rows):
    return pl.BlockSpec((1, rows, width), lambda b, s: (b, s, 0))


def _params():
    return pltpu.CompilerParams(dimension_semantics=("arbitrary", "arbitrary"),
                                vmem_limit_bytes=VMEM_LIMIT)


def kernel(x, norm_mix_g, w_in, conv_a_w, gdn_conv_w, gdn_A_log, gdn_dt_bias, gdn_norm_g,
           w_a_out, w_b_out, w_o, norm_ffn_g, w_up, ffn_conv_w, w_down, norm_final_g):
    B, S, D = x.shape
    depth = w_in.shape[0]
    ts = SEQ_TILE
    n_tiles = S // ts
    grid = (B, n_tiles)
    d_ff = w_down.shape[1]
    tok = jax.ShapeDtypeStruct((B, S, D), F32)
    n_delta, n_merge = len(DELTA_FIELDS) * D, len(MERGE_FIELDS) * D
    pairs = lambda width: jax.ShapeDtypeStruct((B, S // 2, width), jnp.int32)
    cur_tile = lambda b, s: (b, jnp.minimum(s, n_tiles - 1), 0)
    prev_tile = lambda b, s: (b, jnp.maximum(s - 1, 0), 0)
    hbm = pl.BlockSpec(memory_space=pl.ANY)
    pairs_vmem = lambda k, n: pltpu.VMEM((k // 2, n), jnp.int32)

    for l in range(depth):
        w_in_t = jnp.swapaxes(w_in, 1, 2)
        alog = jnp.pad(gdn_A_log[l], (0, LANES - HEADS)).reshape(1, LANES)
        dtb = jnp.pad(gdn_dt_bias[l], (0, LANES - HEADS)).reshape(1, LANES)

        delta_in, merge_in, gb = pl.pallas_call(
            functools.partial(_in_proj_kernel, layer=l),
            grid=grid,
            in_specs=[_tile_spec(D, ts), _const_spec((1, D)), hbm, hbm,
                      _const_spec((CONV_K, D)), _const_spec((GDN_CONV_K, 3 * D)),
                      _const_spec((1, LANES)), _const_spec((1, LANES))],
            out_specs=[_tile_spec(n_delta, ts // 2), _tile_spec(n_merge, ts // 2), _tile_spec(LANES, ts)],
            out_shape=[pairs(n_delta), pairs(n_merge), jax.ShapeDtypeStruct((B, S, LANES), F32)],
            scratch_shapes=[pltpu.VMEM((CARRY_ROWS, D), F32),
                            pltpu.VMEM((CARRY_ROWS, 3 * D), F32),
                            pltpu.VMEM((ts, D), BF16), pltpu.VMEM((ts, D), F32),
                            pairs_vmem(D, 7 * D), pairs_vmem(D, LANES), pairs_vmem(D, 2 * D),
                            pairs_vmem(D, D), pltpu.VMEM((WEIGHT_SLOTS, WEIGHT_BLOCK, D), F32),
                            pltpu.SemaphoreType.DMA((WEIGHT_SLOTS,))],
            compiler_params=_params(),
            name="in_proj",
        )(x, norm_mix_g[l].reshape(1, D), w_in_t, w_a_out,
          conv_a_w[l], gdn_conv_w[l], alog, dtb)

        x = pl.pallas_call(
            functools.partial(_gdn_kernel, layer=l),
            grid=(B, n_tiles + 1),
            in_specs=[pl.BlockSpec((1, ts, D), prev_tile),
                      pl.BlockSpec((1, ts // 2, n_delta), cur_tile),
                      pl.BlockSpec((1, ts // 2, n_merge), prev_tile),
                      pl.BlockSpec((1, ts, LANES), cur_tile), _const_spec((1, HEAD_DIM)), hbm, hbm],
            out_specs=pl.BlockSpec((1, ts, D), prev_tile),
            out_shape=tok,
            scratch_shapes=[pltpu.VMEM((HEADS, HEAD_DIM, HEAD_DIM), F32),
                            pltpu.VMEM((2, ts, D), BF16), pltpu.VMEM((ts, D), BF16),
                            pairs_vmem(D, D), pairs_vmem(D, D),
                            pltpu.VMEM((WEIGHT_SLOTS, WEIGHT_BLOCK, D), F32),
                            pltpu.SemaphoreType.DMA((WEIGHT_SLOTS,))],
            compiler_params=_params(),
            name="gdn_mix",
        )(x, delta_in, merge_in, gb, gdn_norm_g[l].reshape(1, HEAD_DIM), w_b_out, w_o)

        x = pl.pallas_call(
            functools.partial(_ffn_kernel, layer=l, final_norm=(l == depth - 1)),
            grid=grid,
            in_specs=[_tile_spec(D, ts), _const_spec((1, D)), hbm,
                      _const_spec((CONV_K, 2 * d_ff)), hbm, _const_spec((1, D))],
            out_specs=_tile_spec(D, ts),
            out_shape=tok,
            scratch_shapes=[pltpu.VMEM((CARRY_ROWS, 2 * d_ff), F32),
                            pairs_vmem(D, 2 * d_ff), pairs_vmem(d_ff, D),
                            pltpu.VMEM((WEIGHT_SLOTS, D // 16, 2 * d_ff), F32),
                            pltpu.VMEM((WEIGHT_SLOTS, d_ff // 8, D), F32),
                            pltpu.SemaphoreType.DMA((WEIGHT_SLOTS,))],
            compiler_params=_params(),
            name="conv_ffn",
        )(x, norm_ffn_g[l].reshape(1, D), w_up, ffn_conv_w[l], w_down, norm_final_g.reshape(1, D))
    return x
```

```python
import functools

import jax
import jax.numpy as jnp
from jax import lax
from jax.experimental import pallas as pl
from jax.experimental.pallas import tpu as pltpu

D_MODEL = 1024
HEADS = 8
HEAD_DIM = 128
GDN_CONV_K = 4
CONV_K = 3
EPS = 1e-6
LANES = 128
CARRY_ROWS = 8
SEQ_TILE = 256
CHUNK = 64
COL_CHUNK = 256
WEIGHT_BLOCK = 512
WEIGHT_SLOTS = 3
DELTA_FIELDS = ("q", "k", "v", "zs")
MERGE_FIELDS = ("mixa", "sgb")
VMEM_LIMIT = 52 * 1024 * 1024

F32 = jnp.float32
BF16 = jnp.bfloat16


def _dot(a, b):
    return jnp.dot(a, b, preferred_element_type=F32)


def _dot_nt(a, b):
    return lax.dot_general(a, b, (((1,), (1,)), ((), ())), preferred_element_type=F32)


def _dot_tn(a, b):
    return lax.dot_general(a, b, (((0,), (0,)), ((), ())), preferred_element_type=F32)


def _convert_weight(blocks, stage, sem, transpose=False):
    slots = stage.shape[0]

    def copy(i):
        return pltpu.make_async_copy(blocks[i][0], stage.at[i % slots], sem.at[i % slots])

    for i in range(min(slots, len(blocks))):
        copy(i).start()
    for i, (_, dst) in enumerate(blocks):
        copy(i).wait()
        block = stage[i % slots]
        dst[...] = _as_pairs(block.T if transpose else block)
        if i + slots < len(blocks):
            copy(i + slots).start()


def _first_step():
    return (pl.program_id(0) == 0) & (pl.program_id(1) == 0)


def _unpack(x_i32):
    return pltpu.bitcast(x_i32, BF16)


def _as_pairs(x):
    return pltpu.bitcast(x.astype(BF16), jnp.int32)


def _sigmoid(x):
    return 0.5 * jnp.tanh(0.5 * x) + 0.5


def _silu(x):
    h = 0.5 * x
    return h * jnp.tanh(h) + h


def _rmsnorm(x, g):
    return x * lax.rsqrt(jnp.mean(x * x, axis=-1, keepdims=True) + EPS) * g


def _causal_conv(x, carry_ref, w):
    taps = w.shape[0]
    rows = x.shape[0]
    tiles = [carry_ref[...]] + [x[r:r + CARRY_ROWS, :] for r in range(0, rows, CARRY_ROWS)]
    sub = lax.broadcasted_iota(jnp.int32, tiles[0].shape, 0)
    acc = w[taps - 1:taps, :] * x
    for d in range(1, taps):
        rolled = [pltpu.roll(tile, d, 0) for tile in tiles]
        shifted = jnp.concatenate(
            [jnp.where(sub < d, before, here) for before, here in zip(rolled[:-1], rolled[1:])], axis=0)
        acc = acc + w[taps - 1 - d:taps - d, :] * shifted
    carry_ref[...] = tiles[-1]
    return acc


def _field_views(ref, fields):
    return [ref.at[:, :, i * D_MODEL:(i + 1) * D_MODEL] for i in range(len(fields))]


def _merge(a, b):
    out, i, j = [], 0, 0
    while i < len(a) or j < len(b):
        if j >= len(b) or (i < len(a) and i * len(b) <= j * len(a)):
            out.append(a[i])
            i += 1
        else:
            out.append(b[j])
            j += 1
    return out


def _split3(x):
    hi = x.astype(BF16)
    r = x - hi.astype(F32)
    mid = r.astype(BF16)
    lo = (r - mid.astype(F32)).astype(BF16)
    return hi, mid, lo


def _in_proj_kernel(x_ref, ng_ref, win_hbm, waout_hbm,
                    cwa_ref, cwg_ref, alog_ref, dtb_ref,
                    delta_ref, merge_ref, gb_ref,
                    cbuf, qbuf, yabuf, sgabuf, wmain_ref, wab_ref, wgate_ref, waout_ref, stage, sem,
                    *, layer):
    q_ref, k_ref, v_ref, zs_ref = _field_views(delta_ref, DELTA_FIELDS)
    mixa_ref, sgb_ref = _field_views(merge_ref, MERGE_FIELDS)

    @pl.when(_first_step())
    def _():
        rows = stage.shape[1]
        n_main, gate0 = wmain_ref.shape[1], wmain_ref.shape[1] + 2 * HEADS
        _convert_weight([(win_hbm.at[layer, r:r + rows, :], wmain_ref.at[:, r:r + rows])
                         for r in range(0, n_main, rows)], stage, sem, transpose=True)
        _convert_weight([(win_hbm.at[layer, gate0 + r:gate0 + r + rows, :], wgate_ref.at[:, r:r + rows])
                         for r in range(0, wgate_ref.shape[1], rows)], stage, sem, transpose=True)
        _convert_weight([(waout_hbm.at[layer, r:r + rows, :], waout_ref.at[r // 2:(r + rows) // 2, :])
                         for r in range(0, D_MODEL, rows)], stage, sem)
        ab_copy = pltpu.make_async_copy(win_hbm.at[layer, n_main:gate0, :], stage.at[0, 0:2 * HEADS, :],
                                        sem.at[0])
        ab_copy.start()
        ab_copy.wait()
        row = lax.broadcasted_iota(jnp.int32, (LANES, D_MODEL), 0)
        wab_ref[...] = _as_pairs(jnp.where(row < 2 * HEADS, stage[0, 0:LANES, :], 0.0).T)

    @pl.when(pl.program_id(1) == 0)
    def _():
        cbuf[...] = jnp.zeros(cbuf.shape, F32)
        qbuf[...] = jnp.zeros(qbuf.shape, F32)

    h = _rmsnorm(x_ref[0], ng_ref[...]).astype(BF16)

    def proj(block, cols):
        start = block * D_MODEL + cols.start
        return _dot(h, _unpack(wmain_ref[:, start:start + COL_CHUNK]))

    def conv_silu(group, cols):
        gcols = slice(group * D_MODEL + cols.start, group * D_MODEL + cols.stop)
        c = _causal_conv(proj(3 + group, cols), qbuf.at[:, gcols], cwg_ref[:, gcols])
        return _silu(c)

    def l2norm_heads(c, scale, out_ref, cols):
        for lo in range(0, COL_CHUNK, HEAD_DIM):
            ch = c[:, lo:lo + HEAD_DIM]
            inv = lax.rsqrt(jnp.sum(ch * ch, axis=-1, keepdims=True) + EPS)
            out_ref[0, :, cols.start + lo:cols.start + lo + HEAD_DIM] = _as_pairs(
                ch * (inv * scale if scale != 1.0 else inv))

    for start in range(0, D_MODEL, COL_CHUNK):
        cols = slice(start, start + COL_CHUNK)
        l2norm_heads(conv_silu(0, cols), HEAD_DIM ** -0.5, q_ref, cols)
        conv = _causal_conv(proj(1, cols) * proj(2, cols), cbuf.at[:, cols], cwa_ref[:, cols])
        l2norm_heads(conv_silu(1, cols), 1.0, k_ref, cols)
        yabuf[:, cols] = (proj(0, cols) * conv).astype(BF16)
        sgabuf[:, cols] = _sigmoid(_dot(h, _unpack(wgate_ref[:, cols])))
        v_ref[0, :, cols] = _as_pairs(conv_silu(2, cols))
        sgb_ref[0, :, cols] = _as_pairs(_sigmoid(
            _dot(h, _unpack(wgate_ref[:, D_MODEL + start:D_MODEL + start + COL_CHUNK]))))
        pz = proj(6, cols)
        zs_ref[0, :, cols] = _as_pairs(_silu(pz))

    pab = _dot(h, _unpack(wab_ref[...]))
    sp = pab + dtb_ref[...]
    softplus = jnp.maximum(sp, 0.0) + jnp.log1p(jnp.exp(-jnp.abs(sp)))
    lane = lax.broadcasted_iota(jnp.int32, pab.shape, 1)
    gb_ref[0] = jnp.where(lane < HEADS, -jnp.exp(alog_ref[...]) * softplus, _sigmoid(pab))

    mixa_ref[0] = _as_pairs(sgabuf[...] * _dot(yabuf[...], _unpack(waout_ref[...])))


def _gdn_kernel(x_ref, delta_ref, merge_ref, gb_ref, gng_ref, wb_hbm, wo_hbm,
                x1_ref, state, ogbuf, mixbuf, wb_ref, wo_ref, stage, sem, *, layer):
    ts = x_ref.shape[1]
    q_ref, k_ref, v_ref, zs_ref = _field_views(delta_ref, DELTA_FIELDS)
    mixa_ref, sgb_ref = _field_views(merge_ref, MERGE_FIELDS)

    @pl.when(_first_step())
    def _():
        rows = stage.shape[1]
        for w_hbm, w_ref in ((wb_hbm, wb_ref), (wo_hbm, wo_ref)):
            _convert_weight([(w_hbm.at[layer, r:r + rows, :], w_ref.at[r // 2:(r + rows) // 2, :])
                             for r in range(0, D_MODEL, rows)], stage, sem)

    @pl.when(pl.program_id(1) == 0)
    def _():
        state[...] = jnp.zeros(state.shape, F32)
        ogbuf[...] = jnp.zeros(ogbuf.shape, BF16)

    slot = pl.program_id(1) % 2
    og_new = ogbuf.at[slot]
    og_old = ogbuf.at[1 - slot]

    row = lax.broadcasted_iota(jnp.int32, (ts, ts), 0)
    col = lax.broadcasted_iota(jnp.int32, (ts, ts), 1)
    same = (row // CHUNK) == (col // CHUNK)
    tril = same & (col <= row)
    strict = same & (col < row)

    gbv = gb_ref[0]
    tril_b = tril.astype(BF16)
    same_b = same.astype(BF16)
    G = None
    GL = None
    for part in _split3(gbv):
        G = _dot(tril_b, part) if G is None else G + _dot(tril_b, part)
        GL = _dot(same_b, part) if GL is None else GL + _dot(same_b, part)
    GT = G.T
    eG = jnp.exp(G)
    eGLG = jnp.exp(GL - G)
    eGL = jnp.exp(GL)
    gng = gng_ref[...]

    lanes = [slice(hd * HEAD_DIM, (hd + 1) * HEAD_DIM) for hd in range(HEADS)]
    n_chunks = ts // CHUNK
    lane_chunk = lax.broadcasted_iota(jnp.int32, (CHUNK, ts), 1) // CHUNK
    eye_sbs = (lax.broadcasted_iota(jnp.int32, (CHUNK, ts), 0)
               == lax.broadcasted_iota(jnp.int32, (CHUNK, ts), 1) % CHUNK).astype(F32)

    def fold(m):
        acc = m[:CHUNK]
        for c in range(1, n_chunks):
            acc = acc + m[c * CHUNK:(c + 1) * CHUNK]
        return acc

    def unfold(m):
        return jnp.concatenate(
            [jnp.where(lane_chunk == c, m, jnp.zeros_like(m)) for c in range(n_chunks)], axis=0)

    st = [dict() for _ in range(HEADS)]

    def setup(group):
        def prep(hd):
            d = st[hd]
            q = _unpack(q_ref[0, :, lanes[hd]])
            k = _unpack(k_ref[0, :, lanes[hd]])
            v = _unpack(v_ref[0, :, lanes[hd]])
            beta = gbv[:, HEADS + hd:HEADS + hd + 1]
            eg = eG[:, hd:hd + 1]
            kf = k.astype(F32)
            kb = kf * beta
            d["q"], d["k"], d["kb"] = q, k, kb.astype(BF16)
            d["qd"] = (q.astype(F32) * eg).astype(BF16)
            d["kd"] = (kf * eGLG[:, hd:hd + 1]).astype(BF16)
            d["rhs"] = jnp.concatenate([v.astype(F32) * beta, kb * eg], axis=1).astype(BF16)
            d["decay"] = jnp.where(tril, jnp.exp(G[:, hd:hd + 1] - GT[hd:hd + 1, :]), 0.0)

        def products(hd):
            d = st[hd]
            d["L"] = jnp.where(strict, _dot_nt(d["kb"], d["k"]) * d["decay"], 0.0)
            d["A"] = (_dot_nt(d["q"], d["k"]) * d["decay"]).astype(BF16)

        return [functools.partial(f, hd) for hd in group for f in (prep, products)]

    def inverse(group):
        def first(hd):
            d = st[hd]
            l_sbs = fold(d["L"])
            d["T"] = eye_sbs - l_sbs
            d["P"] = _dot(l_sbs.astype(BF16), d["L"].astype(BF16))

        def both(hd):
            d = st[hd]
            p_b = d["P"].astype(BF16)
            tp = _dot(jnp.concatenate([d["T"].astype(BF16), p_b], axis=0), unfold(p_b))
            d["T"] = d["T"] + tp[:CHUNK]
            d["P"] = tp[CHUNK:]

        def last(hd):
            d = st[hd]
            d["T"] = d["T"] + _dot(d["T"].astype(BF16), unfold(d["P"].astype(BF16)))

        def solve(hd):
            d = st[hd]
            uw = _dot(unfold(d["T"].astype(BF16)), d["rhs"])
            d["u"] = uw[:, :HEAD_DIM]
            d["w"] = uw[:, HEAD_DIM:].astype(BF16)

        steps = [functools.partial(first, hd) for hd in group]
        n = 4
        while n < CHUNK:
            steps += [functools.partial(both, hd) for hd in group]
            n *= 2
        return steps + [functools.partial(f, hd) for f in (last, solve) for hd in group]

    def recurrence(group):
        def load(hd):
            st[hd]["S"] = state[hd]

        def apply_state(c, hd):
            d = st[hd]
            rows = slice(c * CHUNK, (c + 1) * CHUNK)
            d["wq"] = _dot(jnp.concatenate([d["w"][rows], d["qd"][rows]], axis=0), d["S"].astype(BF16))

        def update_state(c, hd):
            d = st[hd]
            rows = slice(c * CHUNK, (c + 1) * CHUNK)
            d["vn"] = (d["u"][rows] - d["wq"][:CHUNK]).astype(BF16)
            d["S"] = d["S"] * eGL[c * CHUNK:c * CHUNK + 1, hd:hd + 1] + _dot_tn(d["kd"][rows], d["vn"])

        def output(c, hd):
            d = st[hd]
            rows = slice(c * CHUNK, (c + 1) * CHUNK)
            pair_rows = slice(c * CHUNK // 2, (c + 1) * CHUNK // 2)
            o = d["wq"][CHUNK:] + _dot(d["A"][rows, rows], d["vn"])
            o = o * lax.rsqrt(jnp.mean(o * o, axis=-1, keepdims=True) + EPS) * gng
            zs = _unpack(zs_ref[0, pair_rows, lanes[hd]]).astype(F32)
            og_new[rows, lanes[hd]] = (o * zs).astype(BF16)

        def store(hd):
            state[hd] = st[hd]["S"]

        steps = [functools.partial(load, hd) for hd in group]
        for c in range(n_chunks):
            for f in (apply_state, update_state, output):
                steps += [functools.partial(f, c, hd) for hd in group]
        return steps + [functools.partial(store, hd) for hd in group]

    def merge_epilogue():
        def mix(cols):
            y_b = _dot(og_old[...], _unpack(wb_ref[:, cols]))
            m = _unpack(mixa_ref[0, :, cols]).astype(F32) + _unpack(sgb_ref[0, :, cols]).astype(F32) * y_b
            mixbuf[:, cols] = m.astype(BF16)

        def out(cols):
            x1_ref[0, :, cols] = x_ref[0, :, cols] + _dot(mixbuf[...], _unpack(wo_ref[:, cols]))

        pieces = [slice(start, start + COL_CHUNK) for start in range(0, D_MODEL, COL_CHUNK)]
        return [functools.partial(f, cols) for f in (mix, out) for cols in pieces]

    def run(steps):
        for step in steps:
            step()

    heads = list(range(HEADS))
    run(setup(heads))
    run(_merge(inverse(heads) + recurrence(heads), merge_epilogue()))


def _ffn_kernel(x1_ref, ng_ref, wup_hbm, cw_ref, wdown_hbm, nf_ref, out_ref,
                ubuf, wup_ref, wdown_ref, stage_up, stage_down, sem, *, layer, final_norm):
    d_ff = 2 * wdown_ref.shape[0]

    @pl.when(_first_step())
    def _():
        for w_hbm, w_ref, stage in ((wup_hbm, wup_ref, stage_up), (wdown_hbm, wdown_ref, stage_down)):
            rows = stage.shape[1]
            _convert_weight([(w_hbm.at[layer, r:r + rows, :], w_ref.at[r // 2:(r + rows) // 2, :])
                             for r in range(0, 2 * w_ref.shape[0], rows)], stage, sem)

    @pl.when(pl.program_id(1) == 0)
    def _():
        ubuf[...] = jnp.zeros(ubuf.shape, F32)

    x1 = x1_ref[0]
    h = _rmsnorm(x1, ng_ref[...]).astype(BF16)
    up = _causal_conv(_dot(h, _unpack(wup_ref[...])), ubuf, cw_ref[...])
    act = (_silu(up[:, :d_ff]) * up[:, d_ff:]).astype(BF16)
    x2 = x1 + _dot(act, _unpack(wdown_ref[...]))
    out_ref[0] = _rmsnorm(x2, nf_ref[...]) if final_norm else x2


def _const_spec(shape):
    return pl.BlockSpec(shape, lambda b, s: (0,) * len(shape))


def _tile_spec(width, rows):
    return pl.BlockSpec((1, rows, width), lambda b, s: (b, s, 0))


def _params():
    return pltpu.CompilerParams(dimension_semantics=("arbitrary", "arbitrary"),
                                vmem_limit_bytes=VMEM_LIMIT)


def kernel(x, norm_mix_g, w_in, conv_a_w, gdn_conv_w, gdn_A_log, gdn_dt_bias, gdn_norm_g,
           w_a_out, w_b_out, w_o, norm_ffn_g, w_up, ffn_conv_w, w_down, norm_final_g):
    B, S, D = x.shape
    depth = w_in.shape[0]
    ts = SEQ_TILE
    n_tiles = S // ts
    grid = (B, n_tiles)
    d_ff = w_down.shape[1]
    tok = jax.ShapeDtypeStruct((B, S, D), F32)
    n_delta, n_merge = len(DELTA_FIELDS) * D, len(MERGE_FIELDS) * D
    pairs = lambda width: jax.ShapeDtypeStruct((B, S // 2, width), jnp.int32)
    cur_tile = lambda b, s: (b, jnp.minimum(s, n_tiles - 1), 0)
    prev_tile = lambda b, s: (b, jnp.maximum(s - 1, 0), 0)
    hbm = pl.BlockSpec(memory_space=pl.ANY)
    pairs_vmem = lambda k, n: pltpu.VMEM((k // 2, n), jnp.int32)

    for l in range(depth):
        w_in_t = jnp.swapaxes(w_in, 1, 2)
        alog = jnp.pad(gdn_A_log[l], (0, LANES - HEADS)).reshape(1, LANES)
        dtb = jnp.pad(gdn_dt_bias[l], (0, LANES - HEADS)).reshape(1, LANES)

        delta_in, merge_in, gb = pl.pallas_call(
            functools.partial(_in_proj_kernel, layer=l),
            grid=grid,
            in_specs=[_tile_spec(D, ts), _const_spec((1, D)), hbm, hbm,
                      _const_spec((CONV_K, D)), _const_spec((GDN_CONV_K, 3 * D)),
                      _const_spec((1, LANES)), _const_spec((1, LANES))],
            out_specs=[_tile_spec(n_delta, ts // 2), _tile_spec(n_merge, ts // 2), _tile_spec(LANES, ts)],
            out_shape=[pairs(n_delta), pairs(n_merge), jax.ShapeDtypeStruct((B, S, LANES), F32)],
            scratch_shapes=[pltpu.VMEM((CARRY_ROWS, D), F32),
                            pltpu.VMEM((CARRY_ROWS, 3 * D), F32),
                            pltpu.VMEM((ts, D), BF16), pltpu.VMEM((ts, D), F32),
                            pairs_vmem(D, 7 * D), pairs_vmem(D, LANES), pairs_vmem(D, 2 * D),
                            pairs_vmem(D, D), pltpu.VMEM((WEIGHT_SLOTS, WEIGHT_BLOCK, D), F32),
                            pltpu.SemaphoreType.DMA((WEIGHT_SLOTS,))],
            compiler_params=_params(),
            name="in_proj",
        )(x, norm_mix_g[l].reshape(1, D), w_in_t, w_a_out,
          conv_a_w[l], gdn_conv_w[l], alog, dtb)

        x = pl.pallas_call(
            functools.partial(_gdn_kernel, layer=l),
            grid=(B, n_tiles + 1),
            in_specs=[pl.BlockSpec((1, ts, D), prev_tile),
                      pl.BlockSpec((1, ts // 2, n_delta), cur_tile),
                      pl.BlockSpec((1, ts // 2, n_merge), prev_tile),
                      pl.BlockSpec((1, ts, LANES), cur_tile), _const_spec((1, HEAD_DIM)), hbm, hbm],
            out_specs=pl.BlockSpec((1, ts, D), prev_tile),
            out_shape=tok,
            scratch_shapes=[pltpu.VMEM((HEADS, HEAD_DIM, HEAD_DIM), F32),
                            pltpu.VMEM((2, ts, D), BF16), pltpu.VMEM((ts, D), BF16),
                            pairs_vmem(D, D), pairs_vmem(D, D),
                            pltpu.VMEM((WEIGHT_SLOTS, WEIGHT_BLOCK, D), F32),
                            pltpu.SemaphoreType.DMA((WEIGHT_SLOTS,))],
            compiler_params=_params(),
            name="gdn_mix",
        )(x, delta_in, merge_in, gb, gdn_norm_g[l].reshape(1, HEAD_DIM), w_b_out, w_o)

        x = pl.pallas_call(
            functools.partial(_ffn_kernel, layer=l, final_norm=(l == depth - 1)),
            grid=grid,
            in_specs=[_tile_spec(D, ts), _const_spec((1, D)), hbm,
                      _const_spec((CONV_K, 2 * d_ff)), hbm, _const_spec((1, D))],
            out_specs=_tile_spec(D, ts),
            out_shape=tok,
            scratch_shapes=[pltpu.VMEM((CARRY_ROWS, 2 * d_ff), F32),
                            pairs_vmem(D, 2 * d_ff), pairs_vmem(d_ff, D),
                            pltpu.VMEM((WEIGHT_SLOTS, D // 16, 2 * d_ff), F32),
                            pltpu.VMEM((WEIGHT_SLOTS, d_ff // 8, D), F32),
                            pltpu.SemaphoreType.DMA((WEIGHT_SLOTS,))],
            compiler_params=_params(),
            name="conv_ffn",
        )(x, norm_ffn_g[l].reshape(1, D), w_up, ffn_conv_w[l], w_down, norm_final_g.reshape(1, D))
    return x
```

```python
import functools

import jax
import jax.numpy as jnp
from jax import lax
from jax.experimental import pallas as pl
from jax.experimental.pallas import tpu as pltpu

D_MODEL = 1024
HEADS = 8
HEAD_DIM = 128
GDN_CONV_K = 4
CONV_K = 3
EPS = 1e-6
LANES = 128
CARRY_ROWS = 8
SEQ_TILE = 256
CHUNK = 64
COL_CHUNK = 256
WEIGHT_BLOCK = 512
WEIGHT_SLOTS = 3
DELTA_FIELDS = ("q", "k", "v", "zs")
MERGE_FIELDS = ("mixa", "sgb")
VMEM_LIMIT = 56 * 1024 * 1024

F32 = jnp.float32
BF16 = jnp.bfloat16


def _dot(a, b):
    return jnp.dot(a, b, preferred_element_type=F32)


def _dot_nt(a, b):
    return lax.dot_general(a, b, (((1,), (1,)), ((), ())), preferred_element_type=F32)


def _dot_tn(a, b):
    return lax.dot_general(a, b, (((0,), (0,)), ((), ())), preferred_element_type=F32)


def _convert_weight(blocks, stage, sem, transpose=False):
    slots = stage.shape[0]

    def slot_view(i):
        rows, cols = blocks[i][0].shape
        return stage.at[i % slots, 0:rows, 0:cols]

    def copy(i):
        return pltpu.make_async_copy(blocks[i][0], slot_view(i), sem.at[i % slots])

    for i in range(min(slots, len(blocks))):
        copy(i).start()
    for i, (_, dst) in enumerate(blocks):
        copy(i).wait()
        block = slot_view(i)[...]
        dst[...] = _as_pairs(block.T if transpose else block)
        if i + slots < len(blocks):
            copy(i + slots).start()


def _first_step():
    return (pl.program_id(0) == 0) & (pl.program_id(1) == 0)


def _unpack(x_i32):
    return pltpu.bitcast(x_i32, BF16)


def _as_pairs(x):
    return pltpu.bitcast(x.astype(BF16), jnp.int32)


def _sigmoid(x):
    return 0.5 * jnp.tanh(0.5 * x) + 0.5


def _silu(x):
    h = 0.5 * x
    return h * jnp.tanh(h) + h


def _rmsnorm(x, g):
    return x * lax.rsqrt(jnp.mean(x * x, axis=-1, keepdims=True) + EPS) * g


def _causal_conv(x, carry_ref, w):
    taps = w.shape[0]
    rows = x.shape[0]
    tiles = [carry_ref[...]] + [x[r:r + CARRY_ROWS, :] for r in range(0, rows, CARRY_ROWS)]
    sub = lax.broadcasted_iota(jnp.int32, tiles[0].shape, 0)
    acc = w[taps - 1:taps, :] * x
    for d in range(1, taps):
        rolled = [pltpu.roll(tile, d, 0) for tile in tiles]
        shifted = jnp.concatenate(
            [jnp.where(sub < d, before, here) for before, here in zip(rolled[:-1], rolled[1:])], axis=0)
        acc = acc + w[taps - 1 - d:taps - d, :] * shifted
    carry_ref[...] = tiles[-1]
    return acc


def _field_views(ref, fields):
    return [ref.at[:, :, i * D_MODEL:(i + 1) * D_MODEL] for i in range(len(fields))]


def _merge(a, b):
    out, i, j = [], 0, 0
    while i < len(a) or j < len(b):
        if j >= len(b) or (i < len(a) and i * len(b) <= j * len(a)):
            out.append(a[i])
            i += 1
        else:
            out.append(b[j])
            j += 1
    return out


def _split3(x):
    hi = x.astype(BF16)
    r = x - hi.astype(F32)
    mid = r.astype(BF16)
    lo = (r - mid.astype(F32)).astype(BF16)
    return hi, mid, lo


def _in_proj_kernel(x_ref, ng_ref, win_hbm, waout_hbm,
                    cwa_ref, cwg_ref, alog_ref, dtb_ref,
                    delta_ref, merge_ref, gb_ref,
                    cbuf, qbuf, yabuf, sgabuf, wmain_ref, wab_ref, wgate_ref, waout_ref, stage, sem,
                    *, layer):
    q_ref, k_ref, v_ref, zs_ref = _field_views(delta_ref, DELTA_FIELDS)
    mixa_ref, sgb_ref = _field_views(merge_ref, MERGE_FIELDS)

    @pl.when(_first_step())
    def _():
        rows = stage.shape[1]
        n_main, gate0 = wmain_ref.shape[1], wmain_ref.shape[1] + 2 * HEADS
        _convert_weight([(win_hbm.at[layer, r:r + rows, :], wmain_ref.at[:, r:r + rows])
                         for r in range(0, n_main, rows)], stage, sem, transpose=True)
        _convert_weight([(win_hbm.at[layer, gate0 + r:gate0 + r + rows, :], wgate_ref.at[:, r:r + rows])
                         for r in range(0, wgate_ref.shape[1], rows)], stage, sem, transpose=True)
        _convert_weight([(waout_hbm.at[layer, r:r + rows, :], waout_ref.at[r // 2:(r + rows) // 2, :])
                         for r in range(0, D_MODEL, rows)], stage, sem)
        ab_copy = pltpu.make_async_copy(win_hbm.at[layer, n_main:gate0, :], stage.at[0, 0:2 * HEADS, :],
                                        sem.at[0])
        ab_copy.start()
        ab_copy.wait()
        row = lax.broadcasted_iota(jnp.int32, (LANES, D_MODEL), 0)
        wab_ref[...] = _as_pairs(jnp.where(row < 2 * HEADS, stage[0, 0:LANES, :], 0.0).T)

    @pl.when(pl.program_id(1) == 0)
    def _():
        cbuf[...] = jnp.zeros(cbuf.shape, F32)
        qbuf[...] = jnp.zeros(qbuf.shape, F32)

    h = _rmsnorm(x_ref[0], ng_ref[...]).astype(BF16)

    def proj(block, cols):
        start = block * D_MODEL + cols.start
        return _dot(h, _unpack(wmain_ref[:, start:start + COL_CHUNK]))

    def conv_silu(group, cols):
        gcols = slice(group * D_MODEL + cols.start, group * D_MODEL + cols.stop)
        c = _causal_conv(proj(3 + group, cols), qbuf.at[:, gcols], cwg_ref[:, gcols])
        return _silu(c)

    def l2norm_heads(c, scale, out_ref, cols):
        for lo in range(0, COL_CHUNK, HEAD_DIM):
            ch = c[:, lo:lo + HEAD_DIM]
            inv = lax.rsqrt(jnp.sum(ch * ch, axis=-1, keepdims=True) + EPS)
            out_ref[0, :, cols.start + lo:cols.start + lo + HEAD_DIM] = _as_pairs(
                ch * (inv * scale if scale != 1.0 else inv))

    for start in range(0, D_MODEL, COL_CHUNK):
        cols = slice(start, start + COL_CHUNK)
        l2norm_heads(conv_silu(0, cols), HEAD_DIM ** -0.5, q_ref, cols)
        conv = _causal_conv(proj(1, cols) * proj(2, cols), cbuf.at[:, cols], cwa_ref[:, cols])
        l2norm_heads(conv_silu(1, cols), 1.0, k_ref, cols)
        yabuf[:, cols] = (proj(0, cols) * conv).astype(BF16)
        sgabuf[:, cols] = _sigmoid(_dot(h, _unpack(wgate_ref[:, cols])))
        v_ref[0, :, cols] = _as_pairs(conv_silu(2, cols))
        sgb_ref[0, :, cols] = _as_pairs(_sigmoid(
            _dot(h, _unpack(wgate_ref[:, D_MODEL + start:D_MODEL + start + COL_CHUNK]))))
        pz = proj(6, cols)
        zs_ref[0, :, cols] = _as_pairs(_silu(pz))

    pab = _dot(h, _unpack(wab_ref[...]))
    sp = pab + dtb_ref[...]
    softplus = jnp.maximum(sp, 0.0) + jnp.log1p(jnp.exp(-jnp.abs(sp)))
    lane = lax.broadcasted_iota(jnp.int32, pab.shape, 1)
    gb_ref[0] = jnp.where(lane < HEADS, -jnp.exp(alog_ref[...]) * softplus, _sigmoid(pab))

    mixa_ref[0] = _as_pairs(sgabuf[...] * _dot(yabuf[...], _unpack(waout_ref[...])))


def _gdn_ffn_kernel(x_ref, delta_ref, merge_ref, gb_ref, gng_ref, wb_hbm, wo_hbm,
                    ngf_ref, wup_hbm, cwf_ref, wdown_hbm, nf_ref,
                    out_ref, state, ogbuf, mixbuf, x1buf, hbuf, actbuf, ubuf,
                    wb_ref, wo_ref, wup_ref, wdown_ref, stage, sem, *, layer, final_norm):
    ts = x_ref.shape[1]
    d_ff = actbuf.shape[1]
    q_ref, k_ref, v_ref, zs_ref = _field_views(delta_ref, DELTA_FIELDS)
    mixa_ref, sgb_ref = _field_views(merge_ref, MERGE_FIELDS)

    @pl.when(_first_step())
    def _():
        rows, cols = stage.shape[1:]
        for w_hbm, w_ref in ((wb_hbm, wb_ref), (wo_hbm, wo_ref), (wup_hbm, wup_ref), (wdown_hbm, wdown_ref)):
            k, n = 2 * w_ref.shape[0], w_ref.shape[1]
            _convert_weight([(w_hbm.at[layer, r:min(r + rows, k), c:min(c + cols, n)],
                              w_ref.at[r // 2:min(r + rows, k) // 2, c:min(c + cols, n)])
                             for r in range(0, k, rows) for c in range(0, n, cols)], stage, sem)

    @pl.when(pl.program_id(1) == 0)
    def _():
        state[...] = jnp.zeros(state.shape, F32)
        ogbuf[...] = jnp.zeros(ogbuf.shape, BF16)

    @pl.when(pl.program_id(1) <= 1)
    def _():
        ubuf[...] = jnp.zeros(ubuf.shape, F32)

    slot = pl.program_id(1) % 2
    og_new = ogbuf.at[slot]
    og_old = ogbuf.at[1 - slot]

    row = lax.broadcasted_iota(jnp.int32, (ts, ts), 0)
    col = lax.broadcasted_iota(jnp.int32, (ts, ts), 1)
    same = (row // CHUNK) == (col // CHUNK)
    tril = same & (col <= row)
    strict = same & (col < row)

    gbv = gb_ref[0]
    tril_b = tril.astype(BF16)
    same_b = same.astype(BF16)
    G = None
    GL = None
    for part in _split3(gbv):
        G = _dot(tril_b, part) if G is None else G + _dot(tril_b, part)
        GL = _dot(same_b, part) if GL is None else GL + _dot(same_b, part)
    GT = G.T
    eG = jnp.exp(G)
    eGLG = jnp.exp(GL - G)
    eGL = jnp.exp(GL)
    gng = gng_ref[...]

    lanes = [slice(hd * HEAD_DIM, (hd + 1) * HEAD_DIM) for hd in range(HEADS)]
    n_chunks = ts // CHUNK
    lane_chunk = lax.broadcasted_iota(jnp.int32, (CHUNK, ts), 1) // CHUNK
    eye_sbs = (lax.broadcasted_iota(jnp.int32, (CHUNK, ts), 0)
               == lax.broadcasted_iota(jnp.int32, (CHUNK, ts), 1) % CHUNK).astype(F32)

    def fold(m):
        acc = m[:CHUNK]
        for c in range(1, n_chunks):
            acc = acc + m[c * CHUNK:(c + 1) * CHUNK]
        return acc

    def unfold(m):
        return jnp.concatenate(
            [jnp.where(lane_chunk == c, m, jnp.zeros_like(m)) for c in range(n_chunks)], axis=0)

    st = [dict() for _ in range(HEADS)]

    def setup(group):
        def prep(hd):
            d = st[hd]
            q = _unpack(q_ref[0, :, lanes[hd]])
            k = _unpack(k_ref[0, :, lanes[hd]])
            v = _unpack(v_ref[0, :, lanes[hd]])
            beta = gbv[:, HEADS + hd:HEADS + hd + 1]
            eg = eG[:, hd:hd + 1]
            kf = k.astype(F32)
            kb = kf * beta
            d["q"], d["k"], d["kb"] = q, k, kb.astype(BF16)
            d["qd"] = (q.astype(F32) * eg).astype(BF16)
            d["kd"] = (kf * eGLG[:, hd:hd + 1]).astype(BF16)
            d["rhs"] = jnp.concatenate([v.astype(F32) * beta, kb * eg], axis=1).astype(BF16)
            d["decay"] = jnp.where(tril, jnp.exp(G[:, hd:hd + 1] - GT[hd:hd + 1, :]), 0.0)

        def products(hd):
            d = st[hd]
            d["L"] = jnp.where(strict, _dot_nt(d["kb"], d["k"]) * d["decay"], 0.0)
            d["A"] = (_dot_nt(d["q"], d["k"]) * d["decay"]).astype(BF16)

        return [functools.partial(f, hd) for hd in group for f in (prep, products)]

    def inverse(group):
        def first(hd):
            d = st[hd]
            l_sbs = fold(d["L"])
            d["T"] = eye_sbs - l_sbs
            d["P"] = _dot(l_sbs.astype(BF16), d["L"].astype(BF16))

        def both(hd):
            d = st[hd]
            p_b = d["P"].astype(BF16)
            tp = _dot(jnp.concatenate([d["T"].astype(BF16), p_b], axis=0), unfold(p_b))
            d["T"] = d["T"] + tp[:CHUNK]
            d["P"] = tp[CHUNK:]

        def last(hd):
            d = st[hd]
            d["T"] = d["T"] + _dot(d["T"].astype(BF16), unfold(d["P"].astype(BF16)))

        def solve(hd):
            d = st[hd]
            uw = _dot(unfold(d["T"].astype(BF16)), d["rhs"])
            d["u"] = uw[:, :HEAD_DIM]
            d["w"] = uw[:, HEAD_DIM:].astype(BF16)

        steps = [functools.partial(first, hd) for hd in group]
        n = 4
        while n < CHUNK:
            steps += [functools.partial(both, hd) for hd in group]
            n *= 2
        return steps + [functools.partial(f, hd) for f in (last, solve) for hd in group]

    def recurrence(group):
        def load(hd):
            st[hd]["S"] = state[hd]

        def apply_state(c, hd):
            d = st[hd]
            rows = slice(c * CHUNK, (c + 1) * CHUNK)
            d["wq"] = _dot(jnp.concatenate([d["w"][rows], d["qd"][rows]], axis=0), d["S"].astype(BF16))

        def update_state(c, hd):
            d = st[hd]
            rows = slice(c * CHUNK, (c + 1) * CHUNK)
            d["vn"] = (d["u"][rows] - d["wq"][:CHUNK]).astype(BF16)
            d["S"] = d["S"] * eGL[c * CHUNK:c * CHUNK + 1, hd:hd + 1] + _dot_tn(d["kd"][rows], d["vn"])

        def output(c, hd):
            d = st[hd]
            rows = slice(c * CHUNK, (c + 1) * CHUNK)
            pair_rows = slice(c * CHUNK // 2, (c + 1) * CHUNK // 2)
            o = d["wq"][CHUNK:] + _dot(d["A"][rows, rows], d["vn"])
            o = o * lax.rsqrt(jnp.mean(o * o, axis=-1, keepdims=True) + EPS) * gng
            zs = _unpack(zs_ref[0, pair_rows, lanes[hd]]).astype(F32)
            og_new[rows, lanes[hd]] = (o * zs).astype(BF16)

        def store(hd):
            state[hd] = st[hd]["S"]

        steps = [functools.partial(load, hd) for hd in group]
        for c in range(n_chunks):
            for f in (apply_state, update_state, output):
                steps += [functools.partial(f, c, hd) for hd in group]
        return steps + [functools.partial(store, hd) for hd in group]

    def merge_epilogue():
        def mix(cols):
            y_b = _dot(og_old[...], _unpack(wb_ref[:, cols]))
            m = _unpack(mixa_ref[0, :, cols]).astype(F32) + _unpack(sgb_ref[0, :, cols]).astype(F32) * y_b
            mixbuf[:, cols] = m.astype(BF16)

        def out(cols):
            x1buf[:, cols] = x_ref[0, :, cols] + _dot(mixbuf[...], _unpack(wo_ref[:, cols]))

        return [functools.partial(f, cols) for f in (mix, out) for cols in pieces]

    def conv_ffn():
        sum_sq = []

        def norm():
            hbuf[...] = _rmsnorm(x1buf[...], ngf_ref[...]).astype(BF16)

        def up(cols):
            def branch(c):
                return _causal_conv(_dot(hbuf[...], _unpack(wup_ref[:, c])), ubuf.at[:, c], cwf_ref[:, c])

            gate = branch(cols)
            val = branch(slice(d_ff + cols.start, d_ff + cols.stop))
            actbuf[:, cols] = (_silu(gate) * val).astype(BF16)

        def down(cols):
            x2 = x1buf[:, cols] + _dot(actbuf[...], _unpack(wdown_ref[:, cols]))
            out_ref[0, :, cols] = x2
            sum_sq.append(jnp.sum(x2 * x2, axis=-1, keepdims=True))

        def final():
            inv = lax.rsqrt(sum(sum_sq) * (1.0 / D_MODEL) + EPS)
            out_ref[0] = out_ref[0] * inv * nf_ref[...]

        ff_pieces = [slice(start, start + COL_CHUNK) for start in range(0, d_ff, COL_CHUNK)]
        return ([norm] + [functools.partial(up, cols) for cols in ff_pieces]
                + [functools.partial(down, cols) for cols in pieces] + ([final] if final_norm else []))

    def run(steps):
        for step in steps:
            step()

    pieces = [slice(start, start + COL_CHUNK) for start in range(0, D_MODEL, COL_CHUNK)]
    heads = list(range(HEADS))
    run(_merge(setup(heads) + inverse(heads) + recurrence(heads), merge_epilogue() + conv_ffn()))


def _const_spec(shape):
    return pl.BlockSpec(shape, lambda b, s: (0,) * len(shape))


def _tile_spec(width, rows):
    return pl.BlockSpec((1, rows, width), lambda b, s: (b, s, 0))


def _params():
    return pltpu.CompilerParams(dimension_semantics=("arbitrary", "arbitrary"),
                                vmem_limit_bytes=VMEM_LIMIT)


def kernel(x, norm_mix_g, w_in, conv_a_w, gdn_conv_w, gdn_A_log, gdn_dt_bias, gdn_norm_g,
           w_a_out, w_b_out, w_o, norm_ffn_g, w_up, ffn_conv_w, w_down, norm_final_g):
    B, S, D = x.shape
    depth = w_in.shape[0]
    ts = SEQ_TILE
    n_tiles = S // ts
    grid = (B, n_tiles)
    d_ff = w_down.shape[1]
    tok = jax.ShapeDtypeStruct((B, S, D), F32)
    n_delta, n_merge = len(DELTA_FIELDS) * D, len(MERGE_FIELDS) * D
    pairs = lambda width: jax.ShapeDtypeStruct((B, S // 2, width), jnp.int32)
    cur_tile = lambda b, s: (b, jnp.minimum(s, n_tiles - 1), 0)
    prev_tile = lambda b, s: (b, jnp.maximum(s - 1, 0), 0)
    hbm = pl.BlockSpec(memory_space=pl.ANY)
    pairs_vmem = lambda k, n: pltpu.VMEM((k // 2, n), jnp.int32)

    for l in range(depth):
        w_in_t = jnp.swapaxes(w_in, 1, 2)
        alog = jnp.pad(gdn_A_log[l], (0, LANES - HEADS)).reshape(1, LANES)
        dtb = jnp.pad(gdn_dt_bias[l], (0, LANES - HEADS)).reshape(1, LANES)

        delta_in, merge_in, gb = pl.pallas_call(
            functools.partial(_in_proj_kernel, layer=l),
            grid=grid,
            in_specs=[_tile_spec(D, ts), _const_spec((1, D)), hbm, hbm,
                      _const_spec((CONV_K, D)), _const_spec((GDN_CONV_K, 3 * D)),
                      _const_spec((1, LANES)), _const_spec((1, LANES))],
            out_specs=[_tile_spec(n_delta, ts // 2), _tile_spec(n_merge, ts // 2), _tile_spec(LANES, ts)],
            out_shape=[pairs(n_delta), pairs(n_merge), jax.ShapeDtypeStruct((B, S, LANES), F32)],
            scratch_shapes=[pltpu.VMEM((CARRY_ROWS, D), F32),
                            pltpu.VMEM((CARRY_ROWS, 3 * D), F32),
                            pltpu.VMEM((ts, D), BF16), pltpu.VMEM((ts, D), F32),
                            pairs_vmem(D, 7 * D), pairs_vmem(D, LANES), pairs_vmem(D, 2 * D),
                            pairs_vmem(D, D), pltpu.VMEM((WEIGHT_SLOTS, WEIGHT_BLOCK, D), F32),
                            pltpu.SemaphoreType.DMA((WEIGHT_SLOTS,))],
            compiler_params=_params(),
            name="in_proj",
        )(x, norm_mix_g[l].reshape(1, D), w_in_t, w_a_out,
          conv_a_w[l], gdn_conv_w[l], alog, dtb)

        x = pl.pallas_call(
            functools.partial(_gdn_ffn_kernel, layer=l, final_norm=(l == depth - 1)),
            grid=(B, n_tiles + 1),
            in_specs=[pl.BlockSpec((1, ts, D), prev_tile),
                      pl.BlockSpec((1, ts // 2, n_delta), cur_tile),
                      pl.BlockSpec((1, ts // 2, n_merge), prev_tile),
                      pl.BlockSpec((1, ts, LANES), cur_tile), _const_spec((1, HEAD_DIM)), hbm, hbm,
                      _const_spec((1, D)), hbm, _const_spec((CONV_K, 2 * d_ff)), hbm, _const_spec((1, D))],
            out_specs=pl.BlockSpec((1, ts, D), prev_tile),
            out_shape=tok,
            scratch_shapes=[pltpu.VMEM((HEADS, HEAD_DIM, HEAD_DIM), F32),
                            pltpu.VMEM((2, ts, D), BF16), pltpu.VMEM((ts, D), BF16),
                            pltpu.VMEM((ts, D), F32), pltpu.VMEM((ts, D), BF16),
                            pltpu.VMEM((ts, d_ff), BF16), pltpu.VMEM((CARRY_ROWS, 2 * d_ff), F32),
                            pairs_vmem(D, D), pairs_vmem(D, D),
                            pairs_vmem(D, 2 * d_ff), pairs_vmem(d_ff, D),
                            pltpu.VMEM((WEIGHT_SLOTS, WEIGHT_BLOCK, D), F32),
                            pltpu.SemaphoreType.DMA((WEIGHT_SLOTS,))],
            compiler_params=_params(),
            name="gdn_ffn",
        )(x, delta_in, merge_in, gb, gdn_norm_g[l].reshape(1, HEAD_DIM), w_b_out, w_o,
          norm_ffn_g[l].reshape(1, D), w_up, ffn_conv_w[l], w_down, norm_final_g.reshape(1, D))
    return x
```

```python
import functools

import jax
import jax.numpy as jnp
from jax import lax
from jax.experimental import pallas as pl
from jax.experimental.pallas import tpu as pltpu

D_MODEL = 1024
HEADS = 8
HEAD_DIM = 128
GDN_CONV_K = 4
CONV_K = 3
EPS = 1e-6
LANES = 128
CARRY_ROWS = 8
SEQ_TILE = 256
CHUNK = 64
COL_CHUNK = 256
WEIGHT_BLOCK = 512
WEIGHT_SLOTS = 3
DELTA_FIELDS = ("q", "k", "v", "zs")
MERGE_FIELDS = ("mixa", "sgb")
VMEM_LIMIT = 52 * 1024 * 1024

F32 = jnp.float32
BF16 = jnp.bfloat16


def _dot(a, b):
    return jnp.dot(a, b, preferred_element_type=F32)


def _dot_nt(a, b):
    return lax.dot_general(a, b, (((1,), (1,)), ((), ())), preferred_element_type=F32)


def _dot_tn(a, b):
    return lax.dot_general(a, b, (((0,), (0,)), ((), ())), preferred_element_type=F32)


def _convert_weight(blocks, stage, sem, transpose=False):
    slots = stage.shape[0]

    def copy(i):
        return pltpu.make_async_copy(blocks[i][0], stage.at[i % slots], sem.at[i % slots])

    for i in range(min(slots, len(blocks))):
        copy(i).start()
    for i, (_, dst) in enumerate(blocks):
        copy(i).wait()
        block = stage[i % slots]
        dst[...] = _as_pairs(block.T if transpose else block)
        if i + slots < len(blocks):
            copy(i + slots).start()


def _first_step():
    return (pl.program_id(0) == 0) & (pl.program_id(1) == 0)


def _unpack(x_i32):
    return pltpu.bitcast(x_i32, BF16)


def _as_pairs(x):
    return pltpu.bitcast(x.astype(BF16), jnp.int32)


def _sigmoid(x):
    return 0.5 * jnp.tanh(0.5 * x) + 0.5


def _silu(x):
    h = 0.5 * x
    return h * jnp.tanh(h) + h


def _rmsnorm(x, g):
    return x * lax.rsqrt(jnp.mean(x * x, axis=-1, keepdims=True) + EPS) * g


def _causal_conv(x, carry_ref, w):
    taps = w.shape[0]
    rows = x.shape[0]
    tiles = [carry_ref[...]] + [x[r:r + CARRY_ROWS, :] for r in range(0, rows, CARRY_ROWS)]
    sub = lax.broadcasted_iota(jnp.int32, tiles[0].shape, 0)
    acc = w[taps - 1:taps, :] * x
    for d in range(1, taps):
        rolled = [pltpu.roll(tile, d, 0) for tile in tiles]
        shifted = jnp.concatenate(
            [jnp.where(sub < d, before, here) for before, here in zip(rolled[:-1], rolled[1:])], axis=0)
        acc = acc + w[taps - 1 - d:taps - d, :] * shifted
    carry_ref[...] = tiles[-1]
    return acc


def _field_views(ref, fields):
    return [ref.at[:, :, i * D_MODEL:(i + 1) * D_MODEL] for i in range(len(fields))]


def _merge(a, b):
    out, i, j = [], 0, 0
    while i < len(a) or j < len(b):
        if j >= len(b) or (i < len(a) and i * len(b) <= j * len(a)):
            out.append(a[i])
            i += 1
        else:
            out.append(b[j])
            j += 1
    return out


def _split3(x):
    hi = x.astype(BF16)
    r = x - hi.astype(F32)
    mid = r.astype(BF16)
    lo = (r - mid.astype(F32)).astype(BF16)
    return hi, mid, lo


def _in_proj_kernel(x_ref, ng_ref, win_hbm, waout_hbm,
                    cwa_ref, cwg_ref, alog_ref, dtb_ref,
                    delta_ref, merge_ref, gb_ref,
                    cbuf, qbuf, yabuf, sgabuf, wmain_ref, wab_ref, wgate_ref, waout_ref, stage, sem,
                    *, layer):
    q_ref, k_ref, v_ref, zs_ref = _field_views(delta_ref, DELTA_FIELDS)
    mixa_ref, sgb_ref = _field_views(merge_ref, MERGE_FIELDS)

    @pl.when(_first_step())
    def _():
        rows = stage.shape[1]
        n_main, gate0 = wmain_ref.shape[1], wmain_ref.shape[1] + 2 * HEADS
        _convert_weight([(win_hbm.at[layer, r:r + rows, :], wmain_ref.at[:, r:r + rows])
                         for r in range(0, n_main, rows)], stage, sem, transpose=True)
        _convert_weight([(win_hbm.at[layer, gate0 + r:gate0 + r + rows, :], wgate_ref.at[:, r:r + rows])
                         for r in range(0, wgate_ref.shape[1], rows)], stage, sem, transpose=True)
        _convert_weight([(waout_hbm.at[layer, r:r + rows, :], waout_ref.at[r // 2:(r + rows) // 2, :])
                         for r in range(0, D_MODEL, rows)], stage, sem)
        ab_copy = pltpu.make_async_copy(win_hbm.at[layer, n_main:gate0, :], stage.at[0, 0:2 * HEADS, :],
                                        sem.at[0])
        ab_copy.start()
        ab_copy.wait()
        row = lax.broadcasted_iota(jnp.int32, (LANES, D_MODEL), 0)
        wab_ref[...] = _as_pairs(jnp.where(row < 2 * HEADS, stage[0, 0:LANES, :], 0.0).T)

    @pl.when(pl.program_id(1) == 0)
    def _():
        cbuf[...] = jnp.zeros(cbuf.shape, F32)
        qbuf[...] = jnp.zeros(qbuf.shape, F32)

    h = _rmsnorm(x_ref[0], ng_ref[...]).astype(BF16)

    def proj(block, cols):
        start = block * D_MODEL + cols.start
        return _dot(h, _unpack(wmain_ref[:, start:start + COL_CHUNK]))

    def conv_silu(group, cols):
        gcols = slice(group * D_MODEL + cols.start, group * D_MODEL + cols.stop)
        c = _causal_conv(proj(3 + group, cols), qbuf.at[:, gcols], cwg_ref[:, gcols])
        return _silu(c)

    def l2norm_heads(c, scale, out_ref, cols):
        for lo in range(0, COL_CHUNK, HEAD_DIM):
            ch = c[:, lo:lo + HEAD_DIM]
            inv = lax.rsqrt(jnp.sum(ch * ch, axis=-1, keepdims=True) + EPS)
            out_ref[0, :, cols.start + lo:cols.start + lo + HEAD_DIM] = _as_pairs(
                ch * (inv * scale if scale != 1.0 else inv))

    for start in range(0, D_MODEL, COL_CHUNK):
        cols = slice(start, start + COL_CHUNK)
        l2norm_heads(conv_silu(0, cols), HEAD_DIM ** -0.5, q_ref, cols)
        conv = _causal_conv(proj(1, cols) * proj(2, cols), cbuf.at[:, cols], cwa_ref[:, cols])
        l2norm_heads(conv_silu(1, cols), 1.0, k_ref, cols)
        yabuf[:, cols] = (proj(0, cols) * conv).astype(BF16)
        sgabuf[:, cols] = _sigmoid(_dot(h, _unpack(wgate_ref[:, cols])))
        v_ref[0, :, cols] = _as_pairs(conv_silu(2, cols))
        sgb_ref[0, :, cols] = _as_pairs(_sigmoid(
            _dot(h, _unpack(wgate_ref[:, D_MODEL + start:D_MODEL + start + COL_CHUNK]))))
        pz = proj(6, cols)
        zs_ref[0, :, cols] = _as_pairs(_silu(pz))

    pab = _dot(h, _unpack(wab_ref[...]))
    sp = pab + dtb_ref[...]
    softplus = jnp.maximum(sp, 0.0) + jnp.log1p(jnp.exp(-jnp.abs(sp)))
    lane = lax.broadcasted_iota(jnp.int32, pab.shape, 1)
    gb_ref[0] = jnp.where(lane < HEADS, -jnp.exp(alog_ref[...]) * softplus, _sigmoid(pab))

    mixa_ref[0] = _as_pairs(sgabuf[...] * _dot(yabuf[...], _unpack(waout_ref[...])))


def _gdn_kernel(x_ref, delta_ref, merge_ref, gb_ref, gng_ref, wb_hbm, wo_hbm,
                x1_ref, state, ogbuf, mixbuf, wb_ref, wo_ref, stage, sem, *, layer):
    ts = x_ref.shape[1]
    q_ref, k_ref, v_ref, zs_ref = _field_views(delta_ref, DELTA_FIELDS)
    mixa_ref, sgb_ref = _field_views(merge_ref, MERGE_FIELDS)

    @pl.when(_first_step())
    def _():
        rows = stage.shape[1]
        for w_hbm, w_ref in ((wb_hbm, wb_ref), (wo_hbm, wo_ref)):
            _convert_weight([(w_hbm.at[layer, r:r + rows, :], w_ref.at[r // 2:(r + rows) // 2, :])
                             for r in range(0, D_MODEL, rows)], stage, sem)

    @pl.when(pl.program_id(1) == 0)
    def _():
        state[...] = jnp.zeros(state.shape, F32)
        ogbuf[...] = jnp.zeros(ogbuf.shape, BF16)

    slot = pl.program_id(1) % 2
    og_new = ogbuf.at[slot]
    og_old = ogbuf.at[1 - slot]

    def merge_epilogue():
        def mix(cols):
            y_b = _dot(og_old[...], _unpack(wb_ref[:, cols]))
            m = _unpack(mixa_ref[0, :, cols]).astype(F32) + _unpack(sgb_ref[0, :, cols]).astype(F32) * y_b
            mixbuf[:, cols] = m.astype(BF16)

        def out(cols):
            x1_ref[0, :, cols] = x_ref[0, :, cols] + _dot(mixbuf[...], _unpack(wo_ref[:, cols]))

        pieces = [slice(start, start + COL_CHUNK) for start in range(0, D_MODEL, COL_CHUNK)]
        return [functools.partial(f, cols) for f in (mix, out) for cols in pieces]

    def run(steps):
        for step in steps:
            step()

    def delta_rule_steps():
        row = lax.broadcasted_iota(jnp.int32, (ts, ts), 0)
        col = lax.broadcasted_iota(jnp.int32, (ts, ts), 1)
        same = (row // CHUNK) == (col // CHUNK)
        tril = same & (col <= row)
        strict = same & (col < row)

        gbv = gb_ref[0]
        tril_b = tril.astype(BF16)
        same_b = same.astype(BF16)
        G = None
        GL = None
        for part in _split3(gbv):
            G = _dot(tril_b, part) if G is None else G + _dot(tril_b, part)
            GL = _dot(same_b, part) if GL is None else GL + _dot(same_b, part)
        GT = G.T
        eG = jnp.exp(G)
        eGLG = jnp.exp(GL - G)
        eGL = jnp.exp(GL)
        gng = gng_ref[...]

        lanes = [slice(hd * HEAD_DIM, (hd + 1) * HEAD_DIM) for hd in range(HEADS)]
        n_chunks = ts // CHUNK
        lane_chunk = lax.broadcasted_iota(jnp.int32, (CHUNK, ts), 1) // CHUNK
        eye_sbs = (lax.broadcasted_iota(jnp.int32, (CHUNK, ts), 0)
                   == lax.broadcasted_iota(jnp.int32, (CHUNK, ts), 1) % CHUNK).astype(F32)

        def fold(m):
            acc = m[:CHUNK]
            for c in range(1, n_chunks):
                acc = acc + m[c * CHUNK:(c + 1) * CHUNK]
            return acc

        def unfold(m):
            return jnp.concatenate(
                [jnp.where(lane_chunk == c, m, jnp.zeros_like(m)) for c in range(n_chunks)], axis=0)

        st = [dict() for _ in range(HEADS)]
        heads = range(HEADS)

        def prep(hd):
            d = st[hd]
            q = _unpack(q_ref[0, :, lanes[hd]])
            k = _unpack(k_ref[0, :, lanes[hd]])
            v = _unpack(v_ref[0, :, lanes[hd]])
            beta = gbv[:, HEADS + hd:HEADS + hd + 1]
            eg = eG[:, hd:hd + 1]
            kf = k.astype(F32)
            kb = kf * beta
            d["q"], d["k"], d["kb"] = q, k, kb.astype(BF16)
            d["qd"] = (q.astype(F32) * eg).astype(BF16)
            d["kd"] = (kf * eGLG[:, hd:hd + 1]).astype(BF16)
            d["rhs"] = jnp.concatenate([v.astype(F32) * beta, kb * eg], axis=1).astype(BF16)
            d["decay"] = jnp.where(tril, jnp.exp(G[:, hd:hd + 1] - GT[hd:hd + 1, :]), 0.0)

        def products(hd):
            d = st[hd]
            d["L"] = jnp.where(strict, _dot_nt(d["kb"], d["k"]) * d["decay"], 0.0)
            d["A"] = (_dot_nt(d["q"], d["k"]) * d["decay"]).astype(BF16)

        setup = [functools.partial(f, hd) for hd in heads for f in (prep, products)]

        def first(hd):
            d = st[hd]
            l_sbs = fold(d["L"])
            d["T"] = eye_sbs - l_sbs
            d["P"] = _dot(l_sbs.astype(BF16), d["L"].astype(BF16))

        def both(hd):
            d = st[hd]
            p_b = d["P"].astype(BF16)
            tp = _dot(jnp.concatenate([d["T"].astype(BF16), p_b], axis=0), unfold(p_b))
            d["T"] = d["T"] + tp[:CHUNK]
            d["P"] = tp[CHUNK:]

        def last(hd):
            d = st[hd]
            d["T"] = d["T"] + _dot(d["T"].astype(BF16), unfold(d["P"].astype(BF16)))

        def solve(hd):
            d = st[hd]
            uw = _dot(unfold(d["T"].astype(BF16)), d["rhs"])
            d["u"] = uw[:, :HEAD_DIM]
            d["w"] = uw[:, HEAD_DIM:].astype(BF16)

        inverse = [functools.partial(first, hd) for hd in heads]
        n = 4
        while n < CHUNK:
            inverse += [functools.partial(both, hd) for hd in heads]
            n *= 2
        inverse += [functools.partial(f, hd) for f in (last, solve) for hd in heads]

        def load(hd):
            st[hd]["S"] = state[hd]

        def apply_state(c, hd):
            d = st[hd]
            rows = slice(c * CHUNK, (c + 1) * CHUNK)
            d["wq"] = _dot(jnp.concatenate([d["w"][rows], d["qd"][rows]], axis=0), d["S"].astype(BF16))

        def update_state(c, hd):
            d = st[hd]
            rows = slice(c * CHUNK, (c + 1) * CHUNK)
            d["vn"] = (d["u"][rows] - d["wq"][:CHUNK]).astype(BF16)
            d["S"] = d["S"] * eGL[c * CHUNK:c * CHUNK + 1, hd:hd + 1] + _dot_tn(d["kd"][rows], d["vn"])

        def output(c, hd):
            d = st[hd]
            rows = slice(c * CHUNK, (c + 1) * CHUNK)
            pair_rows = slice(c * CHUNK // 2, (c + 1) * CHUNK // 2)
            o = d["wq"][CHUNK:] + _dot(d["A"][rows, rows], d["vn"])
            o = o * lax.rsqrt(jnp.mean(o * o, axis=-1, keepdims=True) + EPS) * gng
            zs = _unpack(zs_ref[0, pair_rows, lanes[hd]]).astype(F32)
            og_new[rows, lanes[hd]] = (o * zs).astype(BF16)

        def store(hd):
            state[hd] = st[hd]["S"]

        recurrence = [functools.partial(load, hd) for hd in heads]
        for c in range(n_chunks):
            for f in (apply_state, update_state, output):
                recurrence += [functools.partial(f, c, hd) for hd in heads]
        recurrence += [functools.partial(store, hd) for hd in heads]
        return setup, inverse, recurrence

    is_tile = pl.program_id(1) < pl.num_programs(1) - 1

    @pl.when(is_tile)
    def _():
        setup, inverse, recurrence = delta_rule_steps()
        run(setup)
        run(_merge(inverse + recurrence, merge_epilogue()))

    @pl.when(jnp.logical_not(is_tile))
    def _():
        run(merge_epilogue())


def _ffn_kernel(x1_ref, ng_ref, wup_hbm, cw_ref, wdown_hbm, nf_ref, out_ref,
                ubuf, wup_ref, wdown_ref, stage_up, stage_down, sem, *, layer, final_norm):
    d_ff = 2 * wdown_ref.shape[0]

    @pl.when(_first_step())
    def _():
        for w_hbm, w_ref, stage in ((wup_hbm, wup_ref, stage_up), (wdown_hbm, wdown_ref, stage_down)):
            rows = stage.shape[1]
            _convert_weight([(w_hbm.at[layer, r:r + rows, :], w_ref.at[r // 2:(r + rows) // 2, :])
                             for r in range(0, 2 * w_ref.shape[0], rows)], stage, sem)

    @pl.when(pl.program_id(1) == 0)
    def _():
        ubuf[...] = jnp.zeros(ubuf.shape, F32)

    x1 = x1_ref[0]
    h = _rmsnorm(x1, ng_ref[...]).astype(BF16)
    up = _causal_conv(_dot(h, _unpack(wup_ref[...])), ubuf, cw_ref[...])
    act = (_silu(up[:, :d_ff]) * up[:, d_ff:]).astype(BF16)
    x2 = x1 + _dot(act, _unpack(wdown_ref[...]))
    out_ref[0] = _rmsnorm(x2, nf_ref[...]) if final_norm else x2


def _const_spec(shape):
    return pl.BlockSpec(shape, lambda b, s: (0,) * len(shape))


def _tile_spec(width, rows):
    return pl.BlockSpec((1, rows, width), lambda b, s: (b, s, 0))


def _params():
    return pltpu.CompilerParams(dimension_semantics=("arbitrary", "arbitrary"),
                                vmem_limit_bytes=VMEM_LIMIT)


def kernel(x, norm_mix_g, w_in, conv_a_w, gdn_conv_w, gdn_A_log, gdn_dt_bias, gdn_norm_g,
           w_a_out, w_b_out, w_o, norm_ffn_g, w_up, ffn_conv_w, w_down, norm_final_g):
    B, S, D = x.shape
    depth = w_in.shape[0]
    ts = SEQ_TILE
    n_tiles = S // ts
    grid = (B, n_tiles)
    d_ff = w_down.shape[1]
    tok = jax.ShapeDtypeStruct((B, S, D), F32)
    n_delta, n_merge = len(DELTA_FIELDS) * D, len(MERGE_FIELDS) * D
    pairs = lambda width: jax.ShapeDtypeStruct((B, S // 2, width), jnp.int32)
    cur_tile = lambda b, s: (b, jnp.minimum(s, n_tiles - 1), 0)
    prev_tile = lambda b, s: (b, jnp.maximum(s - 1, 0), 0)
    hbm = pl.BlockSpec(memory_space=pl.ANY)
    pairs_vmem = lambda k, n: pltpu.VMEM((k // 2, n), jnp.int32)

    for l in range(depth):
        w_in_t = jnp.swapaxes(w_in, 1, 2)
        alog = jnp.pad(gdn_A_log[l], (0, LANES - HEADS)).reshape(1, LANES)
        dtb = jnp.pad(gdn_dt_bias[l], (0, LANES - HEADS)).reshape(1, LANES)

        delta_in, merge_in, gb = pl.pallas_call(
            functools.partial(_in_proj_kernel, layer=l),
            grid=grid,
            in_specs=[_tile_spec(D, ts), _const_spec((1, D)), hbm, hbm,
                      _const_spec((CONV_K, D)), _const_spec((GDN_CONV_K, 3 * D)),
                      _const_spec((1, LANES)), _const_spec((1, LANES))],
            out_specs=[_tile_spec(n_delta, ts // 2), _tile_spec(n_merge, ts // 2), _tile_spec(LANES, ts)],
            out_shape=[pairs(n_delta), pairs(n_merge), jax.ShapeDtypeStruct((B, S, LANES), F32)],
            scratch_shapes=[pltpu.VMEM((CARRY_ROWS, D), F32),
                            pltpu.VMEM((CARRY_ROWS, 3 * D), F32),
                            pltpu.VMEM((ts, D), BF16), pltpu.VMEM((ts, D), F32),
                            pairs_vmem(D, 7 * D), pairs_vmem(D, LANES), pairs_vmem(D, 2 * D),
                            pairs_vmem(D, D), pltpu.VMEM((WEIGHT_SLOTS, WEIGHT_BLOCK, D), F32),
                            pltpu.SemaphoreType.DMA((WEIGHT_SLOTS,))],
            compiler_params=_params(),
            name="in_proj",
        )(x, norm_mix_g[l].reshape(1, D), w_in_t, w_a_out,
          conv_a_w[l], gdn_conv_w[l], alog, dtb)

        x = pl.pallas_call(
            functools.partial(_gdn_kernel, layer=l),
            grid=(B, n_tiles + 1),
            in_specs=[pl.BlockSpec((1, ts, D), prev_tile),
                      pl.BlockSpec((1, ts // 2, n_delta), cur_tile),
                      pl.BlockSpec((1, ts // 2, n_merge), prev_tile),
                      pl.BlockSpec((1, ts, LANES), cur_tile), _const_spec((1, HEAD_DIM)), hbm, hbm],
            out_specs=pl.BlockSpec((1, ts, D), prev_tile),
            out_shape=tok,
            scratch_shapes=[pltpu.VMEM((HEADS, HEAD_DIM, HEAD_DIM), F32),
                            pltpu.VMEM((2, ts, D), BF16), pltpu.VMEM((ts, D), BF16),
                            pairs_vmem(D, D), pairs_vmem(D, D),
                            pltpu.VMEM((WEIGHT_SLOTS, WEIGHT_BLOCK, D), F32),
                            pltpu.SemaphoreType.DMA((WEIGHT_SLOTS,))],
            compiler_params=_params(),
            name="gdn_mix",
        )(x, delta_in, merge_in, gb, gdn_norm_g[l].reshape(1, HEAD_DIM), w_b_out, w_o)

        x = pl.pallas_call(
            functools.partial(_ffn_kernel, layer=l, final_norm=(l == depth - 1)),
            grid=grid,
            in_specs=[_tile_spec(D, ts), _const_spec((1, D)), hbm,
                      _const_spec((CONV_K, 2 * d_ff)), hbm, _const_spec((1, D))],
            out_specs=_tile_spec(D, ts),
            out_shape=tok,
            scratch_shapes=[pltpu.VMEM((CARRY_ROWS, 2 * d_ff), F32),
                            pairs_vmem(D, 2 * d_ff), pairs_vmem(d_ff, D),
                            pltpu.VMEM((WEIGHT_SLOTS, D // 16, 2 * d_ff), F32),
                            pltpu.VMEM((WEIGHT_SLOTS, d_ff // 8, D), F32),
                            pltpu.SemaphoreType.DMA((WEIGHT_SLOTS,))],
            compiler_params=_params(),
            name="conv_ffn",
        )(x, norm_ffn_g[l].reshape(1, D), w_up, ffn_conv_w[l], w_down, norm_final_g.reshape(1, D))
    return x
```

```python
import functools

import jax
import jax.numpy as jnp
from jax import lax
from jax.experimental import pallas as pl
from jax.experimental.pallas import tpu as pltpu

D_MODEL = 1024
HEADS = 8
HEAD_DIM = 128
GDN_CONV_K = 4
CONV_K = 3
EPS = 1e-6
LANES = 128
CARRY_ROWS = 8
SEQ_TILE = 256
CHUNK = 64
COL_CHUNK = 256
WEIGHT_BLOCK = 512
WEIGHT_SLOTS = 3
DELTA_FIELDS = ("q", "k", "v", "zs")
MERGE_FIELDS = ("mixa", "sgb")
VMEM_LIMIT = 52 * 1024 * 1024

F32 = jnp.float32
BF16 = jnp.bfloat16


def _dot(a, b):
    return jnp.dot(a, b, preferred_element_type=F32)


def _dot_nt(a, b):
    return lax.dot_general(a, b, (((1,), (1,)), ((), ())), preferred_element_type=F32)


def _dot_tn(a, b):
    return lax.dot_general(a, b, (((0,), (0,)), ((), ())), preferred_element_type=F32)


def _convert_weight(blocks, stage, sem, transpose=False):
    slots = stage.shape[0]

    def copy(i):
        return pltpu.make_async_copy(blocks[i][0], stage.at[i % slots], sem.at[i % slots])

    for i in range(min(slots, len(blocks))):
        copy(i).start()
    for i, (_, dst) in enumerate(blocks):
        copy(i).wait()
        block = stage[i % slots]
        dst[...] = _as_pairs(block.T if transpose else block)
        if i + slots < len(blocks):
            copy(i + slots).start()


def _first_step():
    return (pl.program_id(0) == 0) & (pl.program_id(1) == 0)


def _unpack(x_i32):
    return pltpu.bitcast(x_i32, BF16)


def _as_pairs(x):
    return pltpu.bitcast(x.astype(BF16), jnp.int32)


def _sigmoid(x):
    return 0.5 * jnp.tanh(0.5 * x) + 0.5


def _silu(x):
    h = 0.5 * x
    return h * jnp.tanh(h) + h


def _rmsnorm(x, g):
    return x * lax.rsqrt(jnp.mean(x * x, axis=-1, keepdims=True) + EPS) * g


def _causal_conv(x, carry_ref, w):
    taps = w.shape[0]
    rows = x.shape[0]
    tiles = [carry_ref[...]] + [x[r:r + CARRY_ROWS, :] for r in range(0, rows, CARRY_ROWS)]
    sub = lax.broadcasted_iota(jnp.int32, tiles[0].shape, 0)
    acc = w[taps - 1:taps, :] * x
    for d in range(1, taps):
        rolled = [pltpu.roll(tile, d, 0) for tile in tiles]
        shifted = jnp.concatenate(
            [jnp.where(sub < d, before, here) for before, here in zip(rolled[:-1], rolled[1:])], axis=0)
        acc = acc + w[taps - 1 - d:taps - d, :] * shifted
    carry_ref[...] = tiles[-1]
    return acc


def _field_views(ref, fields):
    return [ref.at[:, :, i * D_MODEL:(i + 1) * D_MODEL] for i in range(len(fields))]


def _merge(a, b):
    out, i, j = [], 0, 0
    while i < len(a) or j < len(b):
        if j >= len(b) or (i < len(a) and i * len(b) <= j * len(a)):
            out.append(a[i])
            i += 1
        else:
            out.append(b[j])
            j += 1
    return out


def _split3(x):
    hi = x.astype(BF16)
    r = x - hi.astype(F32)
    mid = r.astype(BF16)
    lo = (r - mid.astype(F32)).astype(BF16)
    return hi, mid, lo


def _in_proj_kernel(x_ref, ng_ref, win_hbm, waout_hbm,
                    cwa_ref, cwg_ref, alog_ref, dtb_ref,
                    delta_ref, merge_ref, gb_ref,
                    cbuf, qbuf, yabuf, sgabuf, wmain_ref, wab_ref, wgate_ref, waout_ref, stage, sem,
                    *, layer):
    q_ref, k_ref, v_ref, zs_ref = _field_views(delta_ref, DELTA_FIELDS)
    mixa_ref, sgb_ref = _field_views(merge_ref, MERGE_FIELDS)

    @pl.when(_first_step())
    def _():
        rows = stage.shape[1]
        n_main, gate0 = wmain_ref.shape[1], wmain_ref.shape[1] + 2 * HEADS
        _convert_weight([(win_hbm.at[layer, r:r + rows, :], wmain_ref.at[:, r:r + rows])
                         for r in range(0, n_main, rows)], stage, sem, transpose=True)
        _convert_weight([(win_hbm.at[layer, gate0 + r:gate0 + r + rows, :], wgate_ref.at[:, r:r + rows])
                         for r in range(0, wgate_ref.shape[1], rows)], stage, sem, transpose=True)
        _convert_weight([(waout_hbm.at[layer, r:r + rows, :], waout_ref.at[r // 2:(r + rows) // 2, :])
                         for r in range(0, D_MODEL, rows)], stage, sem)
        ab_copy = pltpu.make_async_copy(win_hbm.at[layer, n_main:gate0, :], stage.at[0, 0:2 * HEADS, :],
                                        sem.at[0])
        ab_copy.start()
        ab_copy.wait()
        row = lax.broadcasted_iota(jnp.int32, (LANES, D_MODEL), 0)
        wab_ref[...] = _as_pairs(jnp.where(row < 2 * HEADS, stage[0, 0:LANES, :], 0.0).T)

    @pl.when(pl.program_id(1) == 0)
    def _():
        cbuf[...] = jnp.zeros(cbuf.shape, F32)
        qbuf[...] = jnp.zeros(qbuf.shape, F32)

    h = _rmsnorm(x_ref[0], ng_ref[...]).astype(BF16)

    def proj(block, cols):
        start = block * D_MODEL + cols.start
        return _dot(h, _unpack(wmain_ref[:, start:start + COL_CHUNK]))

    def conv_silu(group, cols):
        gcols = slice(group * D_MODEL + cols.start, group * D_MODEL + cols.stop)
        c = _causal_conv(proj(3 + group, cols), qbuf.at[:, gcols], cwg_ref[:, gcols])
        return _silu(c)

    def l2norm_heads(c, scale, out_ref, cols):
        for lo in range(0, COL_CHUNK, HEAD_DIM):
            ch = c[:, lo:lo + HEAD_DIM]
            inv = lax.rsqrt(jnp.sum(ch * ch, axis=-1, keepdims=True) + EPS)
            out_ref[0, :, cols.start + lo:cols.start + lo + HEAD_DIM] = _as_pairs(
                ch * (inv * scale if scale != 1.0 else inv))

    for start in range(0, D_MODEL, COL_CHUNK):
        cols = slice(start, start + COL_CHUNK)
        l2norm_heads(conv_silu(0, cols), HEAD_DIM ** -0.5, q_ref, cols)
        conv = _causal_conv(proj(1, cols) * proj(2, cols), cbuf.at[:, cols], cwa_ref[:, cols])
        l2norm_heads(conv_silu(1, cols), 1.0, k_ref, cols)
        yabuf[:, cols] = (proj(0, cols) * conv).astype(BF16)
        sgabuf[:, cols] = _sigmoid(_dot(h, _unpack(wgate_ref[:, cols])))
        v_ref[0, :, cols] = _as_pairs(conv_silu(2, cols))
        sgb_ref[0, :, cols] = _as_pairs(_sigmoid(
            _dot(h, _unpack(wgate_ref[:, D_MODEL + start:D_MODEL + start + COL_CHUNK]))))
        pz = proj(6, cols)
        zs_ref[0, :, cols] = _as_pairs(_silu(pz))

    pab = _dot(h, _unpack(wab_ref[...]))
    sp = pab + dtb_ref[...]
    softplus = jnp.maximum(sp, 0.0) + jnp.log1p(jnp.exp(-jnp.abs(sp)))
    lane = lax.broadcasted_iota(jnp.int32, pab.shape, 1)
    gb_ref[0] = jnp.where(lane < HEADS, -jnp.exp(alog_ref[...]) * softplus, _sigmoid(pab))

    mixa_ref[0] = _as_pairs(sgabuf[...] * _dot(yabuf[...], _unpack(waout_ref[...])))


def _gdn_kernel(x_ref, delta_ref, merge_ref, gb_ref, gng_ref, wb_hbm, wo_hbm,
                x1_ref, state, ogbuf, mixbuf, wb_ref, wo_ref, stage, sem, *, layer):
    ts = x_ref.shape[1]
    q_ref, k_ref, v_ref, zs_ref = _field_views(delta_ref, DELTA_FIELDS)
    mixa_ref, sgb_ref = _field_views(merge_ref, MERGE_FIELDS)

    @pl.when(_first_step())
    def _():
        rows = stage.shape[1]
        for w_hbm, w_ref in ((wb_hbm, wb_ref), (wo_hbm, wo_ref)):
            _convert_weight([(w_hbm.at[layer, r:r + rows, :], w_ref.at[r // 2:(r + rows) // 2, :])
                             for r in range(0, D_MODEL, rows)], stage, sem)

    @pl.when(pl.program_id(1) == 0)
    def _():
        state[...] = jnp.zeros(state.shape, F32)
        ogbuf[...] = jnp.zeros(ogbuf.shape, BF16)

    slot = pl.program_id(1) % 2
    og_new = ogbuf.at[slot]
    og_old = ogbuf.at[1 - slot]

    def merge_epilogue():
        def mix(cols):
            y_b = _dot(og_old[...], _unpack(wb_ref[:, cols]))
            m = _unpack(mixa_ref[0, :, cols]).astype(F32) + _unpack(sgb_ref[0, :, cols]).astype(F32) * y_b
            mixbuf[:, cols] = m.astype(BF16)

        def out(cols):
            x1_ref[0, :, cols] = x_ref[0, :, cols] + _dot(mixbuf[...], _unpack(wo_ref[:, cols]))

        pieces = [slice(start, start + COL_CHUNK) for start in range(0, D_MODEL, COL_CHUNK)]
        return [functools.partial(f, cols) for f in (mix, out) for cols in pieces]

    def run(steps):
        for step in steps:
            step()

    def delta_rule_steps():
        row = lax.broadcasted_iota(jnp.int32, (ts, ts), 0)
        col = lax.broadcasted_iota(jnp.int32, (ts, ts), 1)
        same = (row // CHUNK) == (col // CHUNK)
        tril = same & (col <= row)
        strict = same & (col < row)

        gbv = gb_ref[0]
        tril_b = tril.astype(BF16)
        same_b = same.astype(BF16)
        G = None
        GL = None
        for part in _split3(gbv):
            G = _dot(tril_b, part) if G is None else G + _dot(tril_b, part)
            GL = _dot(same_b, part) if GL is None else GL + _dot(same_b, part)
        GT = G.T
        eG = jnp.exp(G)
        eGLG = jnp.exp(GL - G)
        eGL = jnp.exp(GL)
        gng = gng_ref[...]

        lanes = [slice(hd * HEAD_DIM, (hd + 1) * HEAD_DIM) for hd in range(HEADS)]
        n_chunks = ts // CHUNK
        lane_chunk = lax.broadcasted_iota(jnp.int32, (CHUNK, ts), 1) // CHUNK
        eye_sbs = (lax.broadcasted_iota(jnp.int32, (CHUNK, ts), 0)
                   == lax.broadcasted_iota(jnp.int32, (CHUNK, ts), 1) % CHUNK).astype(F32)

        def fold(m):
            acc = m[:CHUNK]
            for c in range(1, n_chunks):
                acc = acc + m[c * CHUNK:(c + 1) * CHUNK]
            return acc

        def unfold(m):
            return jnp.concatenate(
                [jnp.where(lane_chunk == c, m, jnp.zeros_like(m)) for c in range(n_chunks)], axis=0)

        st = [dict() for _ in range(HEADS)]
        heads = range(HEADS)

        def prep(hd):
            d = st[hd]
            q = _unpack(q_ref[0, :, lanes[hd]])
            k = _unpack(k_ref[0, :, lanes[hd]])
            v = _unpack(v_ref[0, :, lanes[hd]])
            beta = gbv[:, HEADS + hd:HEADS + hd + 1]
            eg = eG[:, hd:hd + 1]
            kf = k.astype(F32)
            kb = kf * beta
            d["q"], d["k"], d["kb"] = q, k, kb.astype(BF16)
            d["qd"] = (q.astype(F32) * eg).astype(BF16)
            d["kd"] = (kf * eGLG[:, hd:hd + 1]).astype(BF16)
            d["rhs"] = jnp.concatenate([v.astype(F32) * beta, kb * eg], axis=1).astype(BF16)
            d["decay"] = jnp.where(tril, jnp.exp(G[:, hd:hd + 1] - GT[hd:hd + 1, :]), 0.0)

        def products(hd):
            d = st[hd]
            kq = _dot_nt(jnp.concatenate([d["kb"], d["q"]], axis=0), d["k"])
            d["L"] = jnp.where(strict, kq[:ts] * d["decay"], 0.0)
            d["A"] = (kq[ts:] * d["decay"]).astype(BF16)

        setup = [functools.partial(f, hd) for hd in heads for f in (prep, products)]

        def first(hd):
            d = st[hd]
            l_sbs = fold(d["L"])
            d["T"] = eye_sbs - l_sbs
            d["P"] = _dot(l_sbs.astype(BF16), d["L"].astype(BF16))

        def both(hd):
            d = st[hd]
            p_b = d["P"].astype(BF16)
            tp = _dot(jnp.concatenate([d["T"].astype(BF16), p_b], axis=0), unfold(p_b))
            d["T"] = d["T"] + tp[:CHUNK]
            d["P"] = tp[CHUNK:]

        def last(hd):
            d = st[hd]
            d["T"] = d["T"] + _dot(d["T"].astype(BF16), unfold(d["P"].astype(BF16)))

        def solve(hd):
            d = st[hd]
            uw = _dot(unfold(d["T"].astype(BF16)), d["rhs"])
            d["u"] = uw[:, :HEAD_DIM]
            d["w"] = uw[:, HEAD_DIM:].astype(BF16)

        inverse = [functools.partial(first, hd) for hd in heads]
        n = 4
        while n < CHUNK:
            inverse += [functools.partial(both, hd) for hd in heads]
            n *= 2
        inverse += [functools.partial(f, hd) for f in (last, solve) for hd in heads]

        def load(hd):
            st[hd]["S"] = state[hd]

        def apply_state(c, hd):
            d = st[hd]
            rows = slice(c * CHUNK, (c + 1) * CHUNK)
            d["wq"] = _dot(jnp.concatenate([d["w"][rows], d["qd"][rows]], axis=0), d["S"].astype(BF16))

        def update_state(c, hd):
            d = st[hd]
            rows = slice(c * CHUNK, (c + 1) * CHUNK)
            d["vn"] = (d["u"][rows] - d["wq"][:CHUNK]).astype(BF16)
            d["S"] = d["S"] * eGL[c * CHUNK:c * CHUNK + 1, hd:hd + 1] + _dot_tn(d["kd"][rows], d["vn"])

        def output(c, hd):
            d = st[hd]
            rows = slice(c * CHUNK, (c + 1) * CHUNK)
            pair_rows = slice(c * CHUNK // 2, (c + 1) * CHUNK // 2)
            o = d["wq"][CHUNK:] + _dot(d["A"][rows, rows], d["vn"])
            o = o * lax.rsqrt(jnp.mean(o * o, axis=-1, keepdims=True) + EPS) * gng
            zs = _unpack(zs_ref[0, pair_rows, lanes[hd]]).astype(F32)
            og_new[rows, lanes[hd]] = (o * zs).astype(BF16)

        def store(hd):
            state[hd] = st[hd]["S"]

        recurrence = [functools.partial(load, hd) for hd in heads]
        for c in range(n_chunks):
            for f in (apply_state, update_state, output):
                recurrence += [functools.partial(f, c, hd) for hd in heads]
        recurrence += [functools.partial(store, hd) for hd in heads]
        return setup, inverse, recurrence

    is_tile = pl.program_id(1) < pl.num_programs(1) - 1

    @pl.when(is_tile)
    def _():
        setup, inverse, recurrence = delta_rule_steps()
        run(setup)
        run(_merge(inverse + recurrence, merge_epilogue()))

    @pl.when(jnp.logical_not(is_tile))
    def _():
        run(merge_epilogue())


def _ffn_kernel(x1_ref, ng_ref, wup_hbm, cw_ref, wdown_hbm, nf_ref, out_ref,
                ubuf, wup_ref, wdown_ref, stage_up, stage_down, sem, *, layer, final_norm):
    d_ff = 2 * wdown_ref.shape[0]

    @pl.when(_first_step())
    def _():
        for w_hbm, w_ref, stage in ((wup_hbm, wup_ref, stage_up), (wdown_hbm, wdown_ref, stage_down)):
            rows = stage.shape[1]
            _convert_weight([(w_hbm.at[layer, r:r + rows, :], w_ref.at[r // 2:(r + rows) // 2, :])
                             for r in range(0, 2 * w_ref.shape[0], rows)], stage, sem)

    @pl.when(pl.program_id(1) == 0)
    def _():
        ubuf[...] = jnp.zeros(ubuf.shape, F32)

    x1 = x1_ref[0]
    h = _rmsnorm(x1, ng_ref[...]).astype(BF16)
    up = _causal_conv(_dot(h, _unpack(wup_ref[...])), ubuf, cw_ref[...])
    act = (_silu(up[:, :d_ff]) * up[:, d_ff:]).astype(BF16)
    x2 = x1 + _dot(act, _unpack(wdown_ref[...]))
    out_ref[0] = _rmsnorm(x2, nf_ref[...]) if final_norm else x2


def _const_spec(shape):
    return pl.BlockSpec(shape, lambda b, s: (0,) * len(shape))


def _tile_spec(width, rows):
    return pl.BlockSpec((1, rows, width), lambda b, s: (b, s, 0))


def _params():
    return pltpu.CompilerParams(dimension_semantics=("arbitrary", "arbitrary"),
                                vmem_limit_bytes=VMEM_LIMIT)


def kernel(x, norm_mix_g, w_in, conv_a_w, gdn_conv_w, gdn_A_log, gdn_dt_bias, gdn_norm_g,
           w_a_out, w_b_out, w_o, norm_ffn_g, w_up, ffn_conv_w, w_down, norm_final_g):
    B, S, D = x.shape
    depth = w_in.shape[0]
    ts = SEQ_TILE
    n_tiles = S // ts
    grid = (B, n_tiles)
    d_ff = w_down.shape[1]
    tok = jax.ShapeDtypeStruct((B, S, D), F32)
    n_delta, n_merge = len(DELTA_FIELDS) * D, len(MERGE_FIELDS) * D
    pairs = lambda width: jax.ShapeDtypeStruct((B, S // 2, width), jnp.int32)
    cur_tile = lambda b, s: (b, jnp.minimum(s, n_tiles - 1), 0)
    prev_tile = lambda b, s: (b, jnp.maximum(s - 1, 0), 0)
    hbm = pl.BlockSpec(memory_space=pl.ANY)
    pairs_vmem = lambda k, n: pltpu.VMEM((k // 2, n), jnp.int32)

    for l in range(depth):
        w_in_t = jnp.swapaxes(w_in, 1, 2)
        alog = jnp.pad(gdn_A_log[l], (0, LANES - HEADS)).reshape(1, LANES)
        dtb = jnp.pad(gdn_dt_bias[l], (0, LANES - HEADS)).reshape(1, LANES)

        delta_in, merge_in, gb = pl.pallas_call(
            functools.partial(_in_proj_kernel, layer=l),
            grid=grid,
            in_specs=[_tile_spec(D, ts), _const_spec((1, D)), hbm, hbm,
                      _const_spec((CONV_K, D)), _const_spec((GDN_CONV_K, 3 * D)),
                      _const_spec((1, LANES)), _const_spec((1, LANES))],
            out_specs=[_tile_spec(n_delta, ts // 2), _tile_spec(n_merge, ts // 2), _tile_spec(LANES, ts)],
            out_shape=[pairs(n_delta), pairs(n_merge), jax.ShapeDtypeStruct((B, S, LANES), F32)],
            scratch_shapes=[pltpu.VMEM((CARRY_ROWS, D), F32),
                            pltpu.VMEM((CARRY_ROWS, 3 * D), F32),
                            pltpu.VMEM((ts, D), BF16), pltpu.VMEM((ts, D), F32),
                            pairs_vmem(D, 7 * D), pairs_vmem(D, LANES), pairs_vmem(D, 2 * D),
                            pairs_vmem(D, D), pltpu.VMEM((WEIGHT_SLOTS, WEIGHT_BLOCK, D), F32),
                            pltpu.SemaphoreType.DMA((WEIGHT_SLOTS,))],
            compiler_params=_params(),
            name="in_proj",
        )(x, norm_mix_g[l].reshape(1, D), w_in_t, w_a_out,
          conv_a_w[l], gdn_conv_w[l], alog, dtb)

        x = pl.pallas_call(
            functools.partial(_gdn_kernel, layer=l),
            grid=(B, n_tiles + 1),
            in_specs=[pl.BlockSpec((1, ts, D), prev_tile),
                      pl.BlockSpec((1, ts // 2, n_delta), cur_tile),
                      pl.BlockSpec((1, ts // 2, n_merge), prev_tile),
                      pl.BlockSpec((1, ts, LANES), cur_tile), _const_spec((1, HEAD_DIM)), hbm, hbm],
            out_specs=pl.BlockSpec((1, ts, D), prev_tile),
            out_shape=tok,
            scratch_shapes=[pltpu.VMEM((HEADS, HEAD_DIM, HEAD_DIM), F32),
                            pltpu.VMEM((2, ts, D), BF16), pltpu.VMEM((ts, D), BF16),
                            pairs_vmem(D, D), pairs_vmem(D, D),
                            pltpu.VMEM((WEIGHT_SLOTS, WEIGHT_BLOCK, D), F32),
                            pltpu.SemaphoreType.DMA((WEIGHT_SLOTS,))],
            compiler_params=_params(),
            name="gdn_mix",
        )(x, delta_in, merge_in, gb, gdn_norm_g[l].reshape(1, HEAD_DIM), w_b_out, w_o)

        x = pl.pallas_call(
            functools.partial(_ffn_kernel, layer=l, final_norm=(l == depth - 1)),
            grid=grid,
            in_specs=[_tile_spec(D, ts), _const_spec((1, D)), hbm,
                      _const_spec((CONV_K, 2 * d_ff)), hbm, _const_spec((1, D))],
            out_specs=_tile_spec(D, ts),
            out_shape=tok,
            scratch_shapes=[pltpu.VMEM((CARRY_ROWS, 2 * d_ff), F32),
                            pairs_vmem(D, 2 * d_ff), pairs_vmem(d_ff, D),
                            pltpu.VMEM((WEIGHT_SLOTS, D // 16, 2 * d_ff), F32),
                            pltpu.VMEM((WEIGHT_SLOTS, d_ff // 8, D), F32),
                            pltpu.SemaphoreType.DMA((WEIGHT_SLOTS,))],
            compiler_params=_params(),
            name="conv_ffn",
        )(x, norm_ffn_g[l].reshape(1, D), w_up, ffn_conv_w[l], w_down, norm_final_g.reshape(1, D))
    return x
```

```python
import functools

import jax
import jax.numpy as jnp
from jax import lax
from jax.experimental import pallas as pl
from jax.experimental.pallas import tpu as pltpu

D_MODEL = 1024
HEADS = 8
HEAD_DIM = 128
GDN_CONV_K = 4
CONV_K = 3
EPS = 1e-6
LANES = 128
CARRY_ROWS = 8
SEQ_TILE = 256
CHUNK = 64
COL_CHUNK = 256
WEIGHT_BLOCK = 512
WEIGHT_SLOTS = 3
DELTA_FIELDS = ("q", "k", "v", "zs")
MERGE_FIELDS = ("mixa", "sgb")
VMEM_LIMIT = 52 * 1024 * 1024

F32 = jnp.float32
BF16 = jnp.bfloat16


def _dot(a, b):
    return jnp.dot(a, b, preferred_element_type=F32)


def _dot_nt(a, b):
    return lax.dot_general(a, b, (((1,), (1,)), ((), ())), preferred_element_type=F32)


def _dot_tn(a, b):
    return lax.dot_general(a, b, (((0,), (0,)), ((), ())), preferred_element_type=F32)


def _convert_weight(blocks, stage, sem, transpose=False):
    slots = stage.shape[0]

    def copy(i):
        return pltpu.make_async_copy(blocks[i][0], stage.at[i % slots], sem.at[i % slots])

    for i in range(min(slots, len(blocks))):
        copy(i).start()
    for i, (_, dst) in enumerate(blocks):
        copy(i).wait()
        block = stage[i % slots]
        dst[...] = _as_pairs(block.T if transpose else block)
        if i + slots < len(blocks):
            copy(i + slots).start()


def _first_step():
    return (pl.program_id(0) == 0) & (pl.program_id(1) == 0)


def _unpack(x_i32):
    return pltpu.bitcast(x_i32, BF16)


def _as_pairs(x):
    return pltpu.bitcast(x.astype(BF16), jnp.int32)


def _sigmoid(x):
    return 0.5 * jnp.tanh(0.5 * x) + 0.5


def _silu(x):
    h = 0.5 * x
    return h * jnp.tanh(h) + h


def _rmsnorm(x, g):
    return x * lax.rsqrt(jnp.mean(x * x, axis=-1, keepdims=True) + EPS) * g


def _causal_conv(x, carry_ref, w):
    taps = w.shape[0]
    rows = x.shape[0]
    tiles = [carry_ref[...]] + [x[r:r + CARRY_ROWS, :] for r in range(0, rows, CARRY_ROWS)]
    sub = lax.broadcasted_iota(jnp.int32, tiles[0].shape, 0)
    acc = w[taps - 1:taps, :] * x
    for d in range(1, taps):
        rolled = [pltpu.roll(tile, d, 0) for tile in tiles]
        shifted = jnp.concatenate(
            [jnp.where(sub < d, before, here) for before, here in zip(rolled[:-1], rolled[1:])], axis=0)
        acc = acc + w[taps - 1 - d:taps - d, :] * shifted
    carry_ref[...] = tiles[-1]
    return acc


def _field_views(ref, fields):
    return [ref.at[:, :, i * D_MODEL:(i + 1) * D_MODEL] for i in range(len(fields))]


def _merge(a, b):
    out, i, j = [], 0, 0
    while i < len(a) or j < len(b):
        if j >= len(b) or (i < len(a) and i * len(b) <= j * len(a)):
            out.append(a[i])
            i += 1
        else:
            out.append(b[j])
            j += 1
    return out


def _split3(x):
    hi = x.astype(BF16)
    r = x - hi.astype(F32)
    mid = r.astype(BF16)
    lo = (r - mid.astype(F32)).astype(BF16)
    return hi, mid, lo


def _in_proj_kernel(x_ref, ng_ref, win_hbm, waout_hbm,
                    cwa_ref, cwg_ref, alog_ref, dtb_ref,
                    delta_ref, merge_ref, gb_ref,
                    cbuf, qbuf, yabuf, sgabuf, wmain_ref, wab_ref, wgate_ref, waout_ref, stage, sem,
                    *, layer):
    q_ref, k_ref, v_ref, zs_ref = _field_views(delta_ref, DELTA_FIELDS)
    mixa_ref, sgb_ref = _field_views(merge_ref, MERGE_FIELDS)

    @pl.when(_first_step())
    def _():
        rows = stage.shape[1]
        n_main, gate0 = wmain_ref.shape[1], wmain_ref.shape[1] + 2 * HEADS
        _convert_weight([(win_hbm.at[layer, r:r + rows, :], wmain_ref.at[:, r:r + rows])
                         for r in range(0, n_main, rows)], stage, sem, transpose=True)
        _convert_weight([(win_hbm.at[layer, gate0 + r:gate0 + r + rows, :], wgate_ref.at[:, r:r + rows])
                         for r in range(0, wgate_ref.shape[1], rows)], stage, sem, transpose=True)
        _convert_weight([(waout_hbm.at[layer, r:r + rows, :], waout_ref.at[r // 2:(r + rows) // 2, :])
                         for r in range(0, D_MODEL, rows)], stage, sem)
        ab_copy = pltpu.make_async_copy(win_hbm.at[layer, n_main:gate0, :], stage.at[0, 0:2 * HEADS, :],
                                        sem.at[0])
        ab_copy.start()
        ab_copy.wait()
        row = lax.broadcasted_iota(jnp.int32, (LANES, D_MODEL), 0)
        wab_ref[...] = _as_pairs(jnp.where(row < 2 * HEADS, stage[0, 0:LANES, :], 0.0).T)

    @pl.when(pl.program_id(1) == 0)
    def _():
        cbuf[...] = jnp.zeros(cbuf.shape, F32)
        qbuf[...] = jnp.zeros(qbuf.shape, F32)

    h = _rmsnorm(x_ref[0], ng_ref[...]).astype(BF16)

    def proj(block, cols):
        start = block * D_MODEL + cols.start
        return _dot(h, _unpack(wmain_ref[:, start:start + COL_CHUNK]))

    def conv_silu(group, cols):
        gcols = slice(group * D_MODEL + cols.start, group * D_MODEL + cols.stop)
        c = _causal_conv(proj(3 + group, cols), qbuf.at[:, gcols], cwg_ref[:, gcols])
        return _silu(c)

    def l2norm_heads(c, scale, out_ref, cols):
        for lo in range(0, COL_CHUNK, HEAD_DIM):
            ch = c[:, lo:lo + HEAD_DIM]
            inv = lax.rsqrt(jnp.sum(ch * ch, axis=-1, keepdims=True) + EPS)
            out_ref[0, :, cols.start + lo:cols.start + lo + HEAD_DIM] = _as_pairs(
                ch * (inv * scale if scale != 1.0 else inv))

    for start in range(0, D_MODEL, COL_CHUNK):
        cols = slice(start, start + COL_CHUNK)
        l2norm_heads(conv_silu(0, cols), HEAD_DIM ** -0.5, q_ref, cols)
        conv = _causal_conv(proj(1, cols) * proj(2, cols), cbuf.at[:, cols], cwa_ref[:, cols])
        l2norm_heads(conv_silu(1, cols), 1.0, k_ref, cols)
        yabuf[:, cols] = (proj(0, cols) * conv).astype(BF16)
        sgabuf[:, cols] = _sigmoid(_dot(h, _unpack(wgate_ref[:, cols])))
        v_ref[0, :, cols] = _as_pairs(conv_silu(2, cols))
        sgb_ref[0, :, cols] = _as_pairs(_sigmoid(
            _dot(h, _unpack(wgate_ref[:, D_MODEL + start:D_MODEL + start + COL_CHUNK]))))
        pz = proj(6, cols)
        zs_ref[0, :, cols] = _as_pairs(_silu(pz))

    pab = _dot(h, _unpack(wab_ref[...]))
    sp = pab + dtb_ref[...]
    softplus = jnp.maximum(sp, 0.0) + jnp.log1p(jnp.exp(-jnp.abs(sp)))
    lane = lax.broadcasted_iota(jnp.int32, pab.shape, 1)
    gb_ref[0] = jnp.where(lane < HEADS, -jnp.exp(alog_ref[...]) * softplus, _sigmoid(pab))

    mixa_ref[0] = _as_pairs(sgabuf[...] * _dot(yabuf[...], _unpack(waout_ref[...])))


def _gdn_kernel(x_ref, delta_ref, merge_ref, gb_ref, gng_ref, wb_hbm, wo_hbm,
                x1_ref, state, ogbuf, mixbuf, wb_ref, wo_ref, stage, sem, *, layer):
    ts = x_ref.shape[1]
    q_ref, k_ref, v_ref, zs_ref = _field_views(delta_ref, DELTA_FIELDS)
    mixa_ref, sgb_ref = _field_views(merge_ref, MERGE_FIELDS)

    @pl.when(_first_step())
    def _():
        rows = stage.shape[1]
        for w_hbm, w_ref in ((wb_hbm, wb_ref), (wo_hbm, wo_ref)):
            _convert_weight([(w_hbm.at[layer, r:r + rows, :], w_ref.at[r // 2:(r + rows) // 2, :])
                             for r in range(0, D_MODEL, rows)], stage, sem)

    @pl.when(pl.program_id(1) == 0)
    def _():
        state[...] = jnp.zeros(state.shape, F32)
        ogbuf[...] = jnp.zeros(ogbuf.shape, BF16)

    slot = pl.program_id(1) % 2
    og_new = ogbuf.at[slot]
    og_old = ogbuf.at[1 - slot]

    def merge_epilogue():
        def mix(cols):
            y_b = _dot(og_old[...], _unpack(wb_ref[:, cols]))
            m = _unpack(mixa_ref[0, :, cols]).astype(F32) + _unpack(sgb_ref[0, :, cols]).astype(F32) * y_b
            mixbuf[:, cols] = m.astype(BF16)

        def out(cols):
            x1_ref[0, :, cols] = x_ref[0, :, cols] + _dot(mixbuf[...], _unpack(wo_ref[:, cols]))

        pieces = [slice(start, start + COL_CHUNK) for start in range(0, D_MODEL, COL_CHUNK)]
        return tuple([functools.partial(f, cols) for cols in pieces] for f in (mix, out))

    def run(steps):
        for step in steps:
            step()

    def delta_rule_steps():
        row = lax.broadcasted_iota(jnp.int32, (ts, ts), 0)
        col = lax.broadcasted_iota(jnp.int32, (ts, ts), 1)
        same = (row // CHUNK) == (col // CHUNK)
        tril = same & (col <= row)
        strict = same & (col < row)

        gbv = gb_ref[0]
        tril_b = tril.astype(BF16)
        G = None
        for part in _split3(gbv):
            G = _dot(tril_b, part) if G is None else G + _dot(tril_b, part)
        GL = jnp.concatenate([jnp.broadcast_to(G[c * CHUNK + CHUNK - 1:(c + 1) * CHUNK, :], (CHUNK, LANES))
                              for c in range(ts // CHUNK)], axis=0)
        GT = G.T
        eG = jnp.exp(G)
        eGLG = jnp.exp(GL - G)
        eGL = jnp.exp(GL)
        gng = gng_ref[...]

        lanes = [slice(hd * HEAD_DIM, (hd + 1) * HEAD_DIM) for hd in range(HEADS)]
        n_chunks = ts // CHUNK
        lane_chunk = lax.broadcasted_iota(jnp.int32, (CHUNK, ts), 1) // CHUNK
        eye_sbs = (lax.broadcasted_iota(jnp.int32, (CHUNK, ts), 0)
                   == lax.broadcasted_iota(jnp.int32, (CHUNK, ts), 1) % CHUNK).astype(F32)

        def fold(m):
            acc = m[:CHUNK]
            for c in range(1, n_chunks):
                acc = acc + m[c * CHUNK:(c + 1) * CHUNK]
            return acc

        def unfold(m):
            return jnp.concatenate(
                [jnp.where(lane_chunk == c, m, jnp.zeros_like(m)) for c in range(n_chunks)], axis=0)

        st = [dict() for _ in range(HEADS)]
        heads = range(HEADS)

        def prep(hd):
            d = st[hd]
            q = _unpack(q_ref[0, :, lanes[hd]])
            k = _unpack(k_ref[0, :, lanes[hd]])
            v = _unpack(v_ref[0, :, lanes[hd]])
            beta = gbv[:, HEADS + hd:HEADS + hd + 1]
            eg = eG[:, hd:hd + 1]
            kf = k.astype(F32)
            kb = kf * beta
            d["q"], d["k"], d["kb"] = q, k, kb.astype(BF16)
            d["qd"] = (q.astype(F32) * eg).astype(BF16)
            d["kd"] = (kf * eGLG[:, hd:hd + 1]).astype(BF16)
            d["rhs"] = jnp.concatenate([v.astype(F32) * beta, kb * eg], axis=1).astype(BF16)
            d["decay"] = jnp.where(tril, jnp.exp(G[:, hd:hd + 1] - GT[hd:hd + 1, :]), 0.0)

        def products(hd):
            d = st[hd]
            kq = _dot_nt(jnp.concatenate([d["kb"], d["q"]], axis=0), d["k"])
            d["L"] = jnp.where(strict, kq[:ts] * d["decay"], 0.0)
            d["A"] = (kq[ts:] * d["decay"]).astype(BF16)

        setup = [functools.partial(f, hd) for hd in heads for f in (prep, products)]

        def first(hd):
            d = st[hd]
            l_sbs = fold(d["L"])
            d["T"] = eye_sbs - l_sbs
            d["P"] = _dot(l_sbs.astype(BF16), d["L"].astype(BF16))

        def both(hd):
            d = st[hd]
            p_b = d["P"].astype(BF16)
            tp = _dot(jnp.concatenate([d["T"].astype(BF16), p_b], axis=0), unfold(p_b))
            d["T"] = d["T"] + tp[:CHUNK]
            d["P"] = tp[CHUNK:]

        def last(hd):
            d = st[hd]
            d["T"] = d["T"] + _dot(d["T"].astype(BF16), unfold(d["P"].astype(BF16)))

        def solve(hd):
            d = st[hd]
            uw = _dot(unfold(d["T"].astype(BF16)), d["rhs"])
            d["u"] = uw[:, :HEAD_DIM]
            d["w"] = uw[:, HEAD_DIM:].astype(BF16)

        inverse = [functools.partial(first, hd) for hd in heads]
        n = 4
        while n < CHUNK:
            inverse += [functools.partial(both, hd) for hd in heads]
            n *= 2
        inverse += [functools.partial(f, hd) for f in (last, solve) for hd in heads]

        def load(hd):
            st[hd]["S"] = state[hd]

        def apply_state(c, hd):
            d = st[hd]
            rows = slice(c * CHUNK, (c + 1) * CHUNK)
            d["wq"] = _dot(jnp.concatenate([d["w"][rows], d["qd"][rows]], axis=0), d["S"].astype(BF16))

        def update_state(c, hd):
            d = st[hd]
            rows = slice(c * CHUNK, (c + 1) * CHUNK)
            d["vn"] = (d["u"][rows] - d["wq"][:CHUNK]).astype(BF16)
            d["S"] = d["S"] * eGL[c * CHUNK:c * CHUNK + 1, hd:hd + 1] + _dot_tn(d["kd"][rows], d["vn"])

        def output(c, hd):
            d = st[hd]
            rows = slice(c * CHUNK, (c + 1) * CHUNK)
            pair_rows = slice(c * CHUNK // 2, (c + 1) * CHUNK // 2)
            o = d["wq"][CHUNK:] + _dot(d["A"][rows, rows], d["vn"])
            o = o * lax.rsqrt(jnp.mean(o * o, axis=-1, keepdims=True) + EPS) * gng
            zs = _unpack(zs_ref[0, pair_rows, lanes[hd]]).astype(F32)
            og_new[rows, lanes[hd]] = (o * zs).astype(BF16)

        def store(hd):
            state[hd] = st[hd]["S"]

        recurrence = [functools.partial(load, hd) for hd in heads]
        for c in range(n_chunks):
            for f in (apply_state, update_state, output):
                recurrence += [functools.partial(f, c, hd) for hd in heads]
        recurrence += [functools.partial(store, hd) for hd in heads]
        return setup, inverse, recurrence

    is_tile = pl.program_id(1) < pl.num_programs(1) - 1

    @pl.when(is_tile)
    def _():
        setup, inverse, recurrence = delta_rule_steps()
        run(setup)
        mix_steps, out_steps = merge_epilogue()
        run(_merge(inverse, mix_steps))
        run(_merge(recurrence, out_steps))

    @pl.when(jnp.logical_not(is_tile))
    def _():
        mix_steps, out_steps = merge_epilogue()
        run(mix_steps + out_steps)


def _ffn_kernel(x1_ref, ng_ref, wup_hbm, cw_ref, wdown_hbm, nf_ref, out_ref,
                ubuf, wup_ref, wdown_ref, stage_up, stage_down, sem, *, layer, final_norm):
    d_ff = 2 * wdown_ref.shape[0]

    @pl.when(_first_step())
    def _():
        for w_hbm, w_ref, stage in ((wup_hbm, wup_ref, stage_up), (wdown_hbm, wdown_ref, stage_down)):
            rows = stage.shape[1]
            _convert_weight([(w_hbm.at[layer, r:r + rows, :], w_ref.at[r // 2:(r + rows) // 2, :])
                             for r in range(0, 2 * w_ref.shape[0], rows)], stage, sem)

    @pl.when(pl.program_id(1) == 0)
    def _():
        ubuf[...] = jnp.zeros(ubuf.shape, F32)

    x1 = x1_ref[0]
    h = _rmsnorm(x1, ng_ref[...]).astype(BF16)
    up = _causal_conv(_dot(h, _unpack(wup_ref[...])), ubuf, cw_ref[...])
    act = (_silu(up[:, :d_ff]) * up[:, d_ff:]).astype(BF16)
    x2 = x1 + _dot(act, _unpack(wdown_ref[...]))
    out_ref[0] = _rmsnorm(x2, nf_ref[...]) if final_norm else x2


def _const_spec(shape):
    return pl.BlockSpec(shape, lambda b, s: (0,) * len(shape))


def _tile_spec(width, rows):
    return pl.BlockSpec((1, rows, width), lambda b, s: (b, s, 0))


def _params():
    return pltpu.CompilerParams(dimension_semantics=("arbitrary", "arbitrary"),
                                vmem_limit_bytes=VMEM_LIMIT)


def kernel(x, norm_mix_g, w_in, conv_a_w, gdn_conv_w, gdn_A_log, gdn_dt_bias, gdn_norm_g,
           w_a_out, w_b_out, w_o, norm_ffn_g, w_up, ffn_conv_w, w_down, norm_final_g):
    B, S, D = x.shape
    depth = w_in.shape[0]
    ts = SEQ_TILE
    n_tiles = S // ts
    grid = (B, n_tiles)
    d_ff = w_down.shape[1]
    tok = jax.ShapeDtypeStruct((B, S, D), F32)
    n_delta, n_merge = len(DELTA_FIELDS) * D, len(MERGE_FIELDS) * D
    pairs = lambda width: jax.ShapeDtypeStruct((B, S // 2, width), jnp.int32)
    cur_tile = lambda b, s: (b, jnp.minimum(s, n_tiles - 1), 0)
    prev_tile = lambda b, s: (b, jnp.maximum(s - 1, 0), 0)
    hbm = pl.BlockSpec(memory_space=pl.ANY)
    pairs_vmem = lambda k, n: pltpu.VMEM((k // 2, n), jnp.int32)

    for l in range(depth):
        w_in_t = jnp.swapaxes(w_in, 1, 2)
        alog = jnp.pad(gdn_A_log[l], (0, LANES - HEADS)).reshape(1, LANES)
        dtb = jnp.pad(gdn_dt_bias[l], (0, LANES - HEADS)).reshape(1, LANES)

        delta_in, merge_in, gb = pl.pallas_call(
            functools.partial(_in_proj_kernel, layer=l),
            grid=grid,
            in_specs=[_tile_spec(D, ts), _const_spec((1, D)), hbm, hbm,
                      _const_spec((CONV_K, D)), _const_spec((GDN_CONV_K, 3 * D)),
                      _const_spec((1, LANES)), _const_spec((1, LANES))],
            out_specs=[_tile_spec(n_delta, ts // 2), _tile_spec(n_merge, ts // 2), _tile_spec(LANES, ts)],
            out_shape=[pairs(n_delta), pairs(n_merge), jax.ShapeDtypeStruct((B, S, LANES), F32)],
            scratch_shapes=[pltpu.VMEM((CARRY_ROWS, D), F32),
                            pltpu.VMEM((CARRY_ROWS, 3 * D), F32),
                            pltpu.VMEM((ts, D), BF16), pltpu.VMEM((ts, D), F32),
                            pairs_vmem(D, 7 * D), pairs_vmem(D, LANES), pairs_vmem(D, 2 * D),
                            pairs_vmem(D, D), pltpu.VMEM((WEIGHT_SLOTS, WEIGHT_BLOCK, D), F32),
                            pltpu.SemaphoreType.DMA((WEIGHT_SLOTS,))],
            compiler_params=_params(),
            name="in_proj",
        )(x, norm_mix_g[l].reshape(1, D), w_in_t, w_a_out,
          conv_a_w[l], gdn_conv_w[l], alog, dtb)

        x = pl.pallas_call(
            functools.partial(_gdn_kernel, layer=l),
            grid=(B, n_tiles + 1),
            in_specs=[pl.BlockSpec((1, ts, D), prev_tile),
                      pl.BlockSpec((1, ts // 2, n_delta), cur_tile),
                      pl.BlockSpec((1, ts // 2, n_merge), prev_tile),
                      pl.BlockSpec((1, ts, LANES), cur_tile), _const_spec((1, HEAD_DIM)), hbm, hbm],
            out_specs=pl.BlockSpec((1, ts, D), prev_tile),
            out_shape=tok,
            scratch_shapes=[pltpu.VMEM((HEADS, HEAD_DIM, HEAD_DIM), F32),
                            pltpu.VMEM((2, ts, D), BF16), pltpu.VMEM((ts, D), BF16),
                            pairs_vmem(D, D), pairs_vmem(D, D),
                            pltpu.VMEM((WEIGHT_SLOTS, WEIGHT_BLOCK, D), F32),
                            pltpu.SemaphoreType.DMA((WEIGHT_SLOTS,))],
            compiler_params=_params(),
            name="gdn_mix",
        )(x, delta_in, merge_in, gb, gdn_norm_g[l].reshape(1, HEAD_DIM), w_b_out, w_o)

        x = pl.pallas_call(
            functools.partial(_ffn_kernel, layer=l, final_norm=(l == depth - 1)),
            grid=grid,
            in_specs=[_tile_spec(D, ts), _const_spec((1, D)), hbm,
                      _const_spec((CONV_K, 2 * d_ff)), hbm, _const_spec((1, D))],
            out_specs=_tile_spec(D, ts),
            out_shape=tok,
            scratch_shapes=[pltpu.VMEM((CARRY_ROWS, 2 * d_ff), F32),
                            pairs_vmem(D, 2 * d_ff), pairs_vmem(d_ff, D),
                            pltpu.VMEM((WEIGHT_SLOTS, D // 16, 2 * d_ff), F32),
                            pltpu.VMEM((WEIGHT_SLOTS, d_ff // 8, D), F32),
                            pltpu.SemaphoreType.DMA((WEIGHT_SLOTS,))],
            compiler_params=_params(),
            name="conv_ffn",
        )(x, norm_ffn_g[l].reshape(1, D), w_up, ffn_conv_w[l], w_down, norm_final_g.reshape(1, D))
    return x
```
